```python
import math
import jax, jax.numpy as jnp
from jax import lax
import numpy as np

D_MODEL = 1024
BATCH = 8
SEQ = 2048
DEPTH = 1

GRID_W = 64
N_DIFF_HEADS = 4
DIFF_HEAD_DIM = 64
DIFF_V_DIM = 2 * DIFF_HEAD_DIM
ROPE_THETA = 500000.0
ROPE_DIM = DIFF_HEAD_DIM // 4
Q_BLOCK = 128
N_NA_HEADS = 8
NA_HEAD_DIM = 64
NA_KR_MAX = 8
NA_KC = 16
A_QK_W = N_DIFF_HEADS * 2 * DIFF_HEAD_DIM
A_V_W = N_DIFF_HEADS * DIFF_V_DIM
B_W = N_NA_HEADS * NA_HEAD_DIM
IN_W = 2 * A_QK_W + A_V_W + 3 * B_W + 2 * D_MODEL
PEER_HEADS = 8
PEER_N_KEYS = 128
PEER_N_EXPERTS = PEER_N_KEYS * PEER_N_KEYS
PEER_QUERY_DIM = 256
PEER_TOPK = 16
PEER_TOKEN_BLOCK = 128
EPS = 1e-6

kernel_name = "hybrid_diffattn_natten_peer_block"


def rmsnorm(x, g):
    xf = x.astype(jnp.float32)
    y = xf * lax.rsqrt(jnp.mean(xf * xf, axis=-1, keepdims=True) + EPS)
    return (y * g.astype(jnp.float32)).astype(x.dtype)


def rope_tables(seq_len):
    inv_freq = ROPE_THETA ** (-jnp.arange(0, ROPE_DIM, 2, dtype=jnp.float32) / ROPE_DIM)
    ang = jnp.arange(seq_len, dtype=jnp.float32)[:, None] * inv_freq[None, :]
    return jnp.cos(ang), jnp.sin(ang)


def apply_partial_rope(x, cos, sin):
    xr, xp = x[..., :ROPE_DIM], x[..., ROPE_DIM:]
    half = ROPE_DIM // 2
    x1, x2 = xr[..., :half], xr[..., half:]
    c, s = cos.astype(x.dtype), sin.astype(x.dtype)
    rot = jnp.concatenate([x1 * c - x2 * s, x2 * c + x1 * s], axis=-1)
    return jnp.concatenate([rot, xp], axis=-1)


def differential_attention(q, k, v, lam, lam_init, subln_g):
    B, S, _ = v.shape
    H, d = N_DIFF_HEADS, DIFF_HEAD_DIM
    q = q.reshape(B, S, H, 2, d).transpose(0, 2, 3, 1, 4)
    k = k.reshape(B, S, H, 2, d).transpose(0, 2, 3, 1, 4)
    v = v.reshape(B, S, H, DIFF_V_DIM).transpose(0, 2, 1, 3)
    cos, sin = rope_tables(S)
    q = apply_partial_rope(q, cos, sin) * (d ** -0.5)
    k = apply_partial_rope(k, cos, sin)
    n_blocks = S // Q_BLOCK
    qb = q.reshape(B, H, 2, n_blocks, Q_BLOCK, d).transpose(3, 0, 1, 2, 4, 5)
    lam_v = lam.astype(v.dtype)

    def block(q_blk):
        s = jnp.einsum('bhmqd,bhmkd->bhmqk', q_blk, k).astype(jnp.float32)
        p = jax.nn.softmax(s, axis=-1).astype(v.dtype)
        a = p[:, :, 0] - lam_v * p[:, :, 1]
        return jnp.einsum('bhqk,bhkv->bhqv', a, v)

    o = lax.map(block, qb)
    o = o.transpose(1, 0, 3, 2, 4).reshape(B, S, H, DIFF_V_DIM)
    o = rmsnorm(o, subln_g) * (1.0 - lam_init)
    return o.reshape(B, S, H * DIFF_V_DIM)


def neighbourhood_attention(q, k, v, rpb):
    B, S, _ = q.shape
    H, d = N_NA_HEADS, NA_HEAD_DIM
    rows = S // GRID_W
    kr = min(NA_KR_MAX, rows)

    def to_grid(t):
        return t.reshape(B, rows, GRID_W, H, d).transpose(0, 3, 1, 2, 4)

    qg = to_grid(q) * (d ** -0.5)
    kg, vg = to_grid(k), to_grid(v)
    cols = jnp.arange(GRID_W)
    col_start = jnp.clip(cols - NA_KC // 2, 0, GRID_W - NA_KC)
    col_idx = col_start[:, None] + jnp.arange(NA_KC)[None, :]
    col_bias_idx = col_idx - cols[:, None] + (NA_KC - 1)

    def row_block(r):
        rs = jnp.clip(r - kr // 2, 0, rows - kr)
        k_rows = lax.dynamic_slice_in_dim(kg, rs, kr, axis=2)
        v_rows = lax.dynamic_slice_in_dim(vg, rs, kr, axis=2)
        k_nb = k_rows[:, :, :, col_idx, :]
        v_nb = v_rows[:, :, :, col_idx, :]
        q_r = lax.dynamic_index_in_dim(qg, r, axis=2, keepdims=False)
        row_bias_idx = rs + jnp.arange(kr) - r + (NA_KR_MAX - 1)
        bias = rpb[:, row_bias_idx[:, None, None], col_bias_idx[None, :, :]]
        bias = bias.transpose(0, 2, 1, 3)
        s = jnp.einsum('bhcd,bhicjd->bhcij', q_r, k_nb) + bias[None]
        s = s.astype(jnp.float32).reshape(B, H, GRID_W, kr * NA_KC)
        p = jax.nn.softmax(s, axis=-1).astype(v.dtype).reshape(B, H, GRID_W, kr, NA_KC)
        return jnp.einsum('bhcij,bhicjd->bhcd', p, v_nb)

    o = lax.map(row_block, jnp.arange(rows))
    return o.transpose(1, 0, 3, 2, 4).reshape(B, S, H * d)


def peer_ffn(x, w_query, sub_keys, expert_u, expert_v):
    B, S, D = x.shape
    TB, PH, K = PEER_TOKEN_BLOCK, PEER_HEADS, PEER_TOPK
    xt = x.reshape(B * S // TB, TB, D)

    def block(xb):
        q = (xb @ w_query).reshape(TB, PH, 2, PEER_QUERY_DIM // 2)
        s = jnp.einsum('thpd,hpkd->thpk', q, sub_keys).astype(jnp.float32)
        v_half, i_half = lax.top_k(s, K)
        cand = v_half[:, :, 0, :, None] + v_half[:, :, 1, None, :]
        cidx = i_half[:, :, 0, :, None] * PEER_N_KEYS + i_half[:, :, 1, None, :]
        top_s, pos = lax.top_k(cand.reshape(TB, PH, K * K), K)
        idx = jnp.take_along_axis(cidx.reshape(TB, PH, K * K), pos, axis=-1)
        g = jax.nn.softmax(top_s, axis=-1).astype(xb.dtype)
        u = expert_u[idx]
        a = jax.nn.gelu(jnp.einsum('td,thkd->thk', xb, u), approximate=False)
        return jnp.einsum('thk,thkd->td', g * a, expert_v[idx])

    return lax.map(block, xt).reshape(B, S, D)


def setup_inputs(seed: int = 0) -> dict:
    key = jax.random.key(seed)
    ks = jax.random.split(key, 20)
    f32 = jnp.float32
    nrm = lambda k, shape, scale: jax.random.normal(k, shape, f32) * scale
    return {
        "x": nrm(ks[0], (BATCH, SEQ, D_MODEL), 1.0),
        "w_in": nrm(ks[1], (DEPTH, D_MODEL, IN_W), D_MODEL ** -0.5),
        "w_proj_a": nrm(ks[2], (DEPTH, A_V_W, D_MODEL), A_V_W ** -0.5),
        "w_proj_b": nrm(ks[3], (DEPTH, B_W, D_MODEL), B_W ** -0.5),
        "w_out": nrm(ks[4], (DEPTH, D_MODEL, D_MODEL), D_MODEL ** -0.5),
        "norm_mix": 1.0 + nrm(ks[5], (DEPTH, D_MODEL), 0.02),
        "norm_ffn": 1.0 + nrm(ks[6], (DEPTH, D_MODEL), 0.02),
        "norm_final": 1.0 + nrm(ks[7], (D_MODEL,), 0.02),
        "lambda_q1": nrm(ks[8], (DEPTH, DIFF_HEAD_DIM), 0.1),
        "lambda_k1": nrm(ks[9], (DEPTH, DIFF_HEAD_DIM), 0.1),
        "lambda_q2": nrm(ks[10], (DEPTH, DIFF_HEAD_DIM), 0.1),
        "lambda_k2": nrm(ks[11], (DEPTH, DIFF_HEAD_DIM), 0.1),
        "diff_subln": 1.0 + nrm(ks[12], (DEPTH, DIFF_V_DIM), 0.02),
        "na_rpb": nrm(ks[13], (DEPTH, N_NA_HEADS, 2 * NA_KR_MAX - 1, 2 * NA_KC - 1), 0.1),
        "peer_w_query": nrm(ks[14], (DEPTH, D_MODEL, PEER_HEADS * PEER_QUERY_DIM), D_MODEL ** -0.5),
        "peer_sub_keys": nrm(ks[15], (DEPTH, PEER_HEADS, 2, PEER_N_KEYS, PEER_QUERY_DIM // 2), (PEER_QUERY_DIM // 2) ** -0.5),
        "peer_u": nrm(ks[16], (DEPTH, PEER_N_EXPERTS, D_MODEL), D_MODEL ** -0.5),
        "peer_v": nrm(ks[17], (DEPTH, PEER_N_EXPERTS, D_MODEL), 0.5),
    }


def reference(x, w_in, w_proj_a, w_proj_b, w_out, norm_mix, norm_ffn, norm_final,
              lambda_q1, lambda_k1, lambda_q2, lambda_k2, diff_subln, na_rpb,
              peer_w_query, peer_sub_keys, peer_u, peer_v):
    split_points = list(np.cumsum([A_QK_W, A_QK_W, A_V_W, B_W, B_W, B_W, D_MODEL])[:])
    h = x
    for l in range(DEPTH):
        lam_init = 0.8 - 0.6 * math.exp(-0.3 * l)
        lam = (jnp.exp(jnp.sum(lambda_q1[l].astype(jnp.float32) * lambda_k1[l].astype(jnp.float32)))
               - jnp.exp(jnp.sum(lambda_q2[l].astype(jnp.float32) * lambda_k2[l].astype(jnp.float32)))
               + lam_init)
        n = rmsnorm(h, norm_mix[l])
        proj = n @ w_in[l]
        qa, ka, va, qb, kb, vb, ga, gb = jnp.split(proj, split_points, axis=-1)
        y_a = differential_attention(qa, ka, va, lam, lam_init, diff_subln[l])
        y_b = neighbourhood_attention(qb, kb, vb, na_rpb[l])
        merged = (jax.nn.sigmoid(ga) * (y_a @ w_proj_a[l])
                  + jax.nn.sigmoid(gb) * (y_b @ w_proj_b[l]))
        h = h + merged @ w_out[l]
        h = h + peer_ffn(rmsnorm(h, norm_ffn[l]), peer_w_query[l], peer_sub_keys[l],
                         peer_u[l], peer_v[l])
    return rmsnorm(h, norm_final)
```

```python
import functools
import math

import jax
import jax.numpy as jnp
import numpy as np
from jax import lax
from jax.experimental import pallas as pl
from jax.experimental.pallas import tpu as pltpu

F32 = jnp.float32
BF16 = jnp.bfloat16

D_MODEL = 1024
GRID_W = 64
N_DIFF_HEADS = 4
DIFF_HEAD_DIM = 64
DIFF_V_DIM = 2 * DIFF_HEAD_DIM
ROPE_THETA = 500000.0
ROPE_DIM = DIFF_HEAD_DIM // 4
N_NA_HEADS = 8
NA_HEAD_DIM = 64
NA_KR_MAX = 8
NA_KC = 16
A_QK_W = N_DIFF_HEADS * 2 * DIFF_HEAD_DIM
A_V_W = N_DIFF_HEADS * DIFF_V_DIM
B_W = N_NA_HEADS * NA_HEAD_DIM
QKV_W = 2 * A_QK_W + A_V_W + 3 * B_W
GATE_W = 2 * D_MODEL
PEER_HEADS = 8
PEER_N_KEYS = 128
PEER_N_EXPERTS = PEER_N_KEYS * PEER_N_KEYS
PEER_QUERY_DIM = 256
PEER_TOPK = 16
EPS = 1e-6

LANES = 128
NEG_BIG = -1e30
MIB = 1024 * 1024

TM_INPROJ = 512
TQ_DIFF = 256
TM_MERGE = 256
TT_PEER = 512
EB_PEER = 1024


def _cparams(sem, vmem_mib):
    return pltpu.CompilerParams(dimension_semantics=sem, vmem_limit_bytes=vmem_mib * MIB)


def _inproj_kernel(x_ref, g_ref, w_ref, c_ref, sa_ref, sb_ref, qkv_ref, gate_ref):
    x = x_ref[...]
    ms = jnp.mean(x * x, axis=-1, keepdims=True)
    n = (x * lax.rsqrt(ms + EPS) * g_ref[...]).astype(BF16)
    chunk = 512
    reps = chunk // LANES
    cos = jnp.tile(c_ref[...], (1, reps))
    sin_up = jnp.tile(sa_ref[...], (1, reps))
    sin_dn = jnp.tile(sb_ref[...], (1, reps))
    half = ROPE_DIM // 2
    for j in range(QKV_W // chunk):
        y = jnp.dot(n, w_ref[:, j * chunk:(j + 1) * chunk], preferred_element_type=F32)
        if j < 2 * A_QK_W // chunk:
            y = (y * cos + pltpu.roll(y, half, axis=1) * sin_up
                 + pltpu.roll(y, chunk - half, axis=1) * sin_dn)
            if j < A_QK_W // chunk:
                y = y * (DIFF_HEAD_DIM ** -0.5)
        qkv_ref[:, j * chunk:(j + 1) * chunk] = y.astype(BF16)
    for j in range(GATE_W // chunk):
        c0 = QKV_W + j * chunk
        gate_ref[:, j * chunk:(j + 1) * chunk] = jnp.dot(
            n, w_ref[:, c0:c0 + chunk], preferred_element_type=F32)


def _inproj(x2, g, w_bf, cos_t, sin_up_t, sin_dn_t, seq):
    t = x2.shape[0]
    tm = TM_INPROJ
    nblk = seq // tm
    in_w = w_bf.shape[1]
    tab_spec = pl.BlockSpec((tm, LANES), lambda i: (i % nblk, 0))
    return pl.pallas_call(
        _inproj_kernel,
        grid=(t // tm,),
        in_specs=[
            pl.BlockSpec((tm, D_MODEL), lambda i: (i, 0)),
            pl.BlockSpec((1, D_MODEL), lambda i: (0, 0)),
            pl.BlockSpec((D_MODEL, in_w), lambda i: (0, 0)),
            tab_spec, tab_spec, tab_spec,
        ],
        out_specs=[
            pl.BlockSpec((tm, QKV_W), lambda i: (i, 0)),
            pl.BlockSpec((tm, GATE_W), lambda i: (i, 0)),
        ],
        out_shape=[
            jax.ShapeDtypeStruct((t, QKV_W), BF16),
            jax.ShapeDtypeStruct((t, GATE_W), F32),
        ],
        compiler_params=_cparams(("arbitrary",), 56),
        name="inproj",
    )(x2, g, w_bf, cos_t, sin_up_t, sin_dn_t)


def _diff_attn_kernel(lam_init, q_ref, k_ref, v_ref, lq1_ref, lk1_ref, lq2_ref, lk2_ref,
                      g_ref, o_ref):
    q = q_ref[...]
    k = k_ref[...]
    lane = lax.broadcasted_iota(jnp.int32, q.shape, 1)
    zero = jnp.zeros_like(q)
    q1 = jnp.where(lane < DIFF_HEAD_DIM, q, zero)
    q2 = jnp.where(lane >= DIFF_HEAD_DIM, q, zero)
    nt = (((1,), (1,)), ((), ()))
    s1 = lax.dot_general(q1, k, nt, preferred_element_type=F32)
    s2 = lax.dot_general(q2, k, nt, preferred_element_type=F32)
    e1 = jnp.exp(s1 - jnp.max(s1, axis=-1, keepdims=True))
    e2 = jnp.exp(s2 - jnp.max(s2, axis=-1, keepdims=True))
    l1 = jnp.sum(e1, axis=-1, keepdims=True)
    l2 = jnp.sum(e2, axis=-1, keepdims=True)
    lam = (jnp.exp(jnp.sum(lq1_ref[...] * lk1_ref[...], axis=-1, keepdims=True))
           - jnp.exp(jnp.sum(lq2_ref[...] * lk2_ref[...], axis=-1, keepdims=True))
           + lam_init)
    a = e1 * (1.0 / l1) - e2 * (lam / l2)
    o = jnp.dot(a.astype(BF16), v_ref[...], preferred_element_type=F32)
    ms = jnp.mean(o * o, axis=-1, keepdims=True)
    o = o * lax.rsqrt(ms + EPS) * g_ref[...] * (1.0 - lam_init)
    o_ref[...] = o.astype(o_ref.dtype)


def _diff_attn(qkv3, lq1, lk1, lq2, lk2, subln_g, lam_init):
    b, s, _ = qkv3.shape
    tq = TQ_DIFF
    hw = 2 * DIFF_HEAD_DIM
    k_blk0 = A_QK_W // hw
    v_blk0 = 2 * A_QK_W // hw
    vec = pl.BlockSpec((1, DIFF_HEAD_DIM), lambda bi, h, i: (0, 0))
    return pl.pallas_call(
        functools.partial(_diff_attn_kernel, lam_init),
        grid=(b, N_DIFF_HEADS, s // tq),
        in_specs=[
            pl.BlockSpec((None, tq, hw), lambda bi, h, i: (bi, i, h)),
            pl.BlockSpec((None, s, hw), lambda bi, h, i: (bi, 0, k_blk0 + h)),
            pl.BlockSpec((None, s, DIFF_V_DIM), lambda bi, h, i: (bi, 0, v_blk0 + h)),
            vec, vec, vec, vec,
            pl.BlockSpec((1, DIFF_V_DIM), lambda bi, h, i: (0, 0)),
        ],
        out_specs=pl.BlockSpec((None, tq, DIFF_V_DIM), lambda bi, h, i: (bi, i, h)),
        out_shape=jax.ShapeDtypeStruct((b, s, A_V_W), BF16),
        compiler_params=_cparams(("arbitrary", "arbitrary", "arbitrary"), 48),
        name="diff_attn",
    )(qkv3, qkv3, qkv3, lq1, lk1, lq2, lk2, subln_g)


def _na_window_start(r, rows, kr):
    return jnp.clip(r - kr // 2, 0, rows - kr)


def _na_kernel(rows, kr, q_ref, k_ref, v_ref, bias_ref, o_ref):
    r = pl.program_id(1)
    start = pl.multiple_of(_na_window_start(r, rows, kr) * GRID_W, GRID_W)
    nkeys = kr * GRID_W
    kwin = k_ref[pl.ds(start, nkeys), :]
    vwin = v_ref[pl.ds(start, nkeys), :]
    q = q_ref[...] * (NA_HEAD_DIM ** -0.5)
    nt = (((1,), (1,)), ((), ()))
    pair_w = 2 * NA_HEAD_DIM
    lane = lax.broadcasted_iota(jnp.int32, (GRID_W, pair_w), 1)
    for hp in range(N_NA_HEADS // 2):
        sl = slice(hp * pair_w, (hp + 1) * pair_w)
        qp, kp, vp = q[:, sl], kwin[:, sl], vwin[:, sl]
        outs = []
        for hh in range(2):
            keep = (lane < NA_HEAD_DIM) if hh == 0 else (lane >= NA_HEAD_DIM)
            qm = jnp.where(keep, qp, jnp.zeros_like(qp))
            s = lax.dot_general(qm, kp, nt, preferred_element_type=F32)
            s = s + bias_ref[2 * hp + hh]
            e = jnp.exp(s - jnp.max(s, axis=-1, keepdims=True))
            p = e * (1.0 / jnp.sum(e, axis=-1, keepdims=True))
            outs.append(jnp.dot(p.astype(BF16), vp, preferred_element_type=F32))
        o_ref[:, sl] = jnp.where(lane < NA_HEAD_DIM, outs[0], outs[1]).astype(o_ref.dtype)


def _na_attn(qkv3, bias_tab, rows, kr):
    b, s, _ = qkv3.shape
    q_blk = (2 * A_QK_W + A_V_W) // B_W
    nkeys = kr * GRID_W

    def bias_idx(bi, r):
        return (0, r - _na_window_start(r, rows, kr), 0, 0)

    return pl.pallas_call(
        functools.partial(_na_kernel, rows, kr),
        grid=(b, rows),
        in_specs=[
            pl.BlockSpec((None, GRID_W, B_W), lambda bi, r: (bi, r, q_blk)),
            pl.BlockSpec((None, s, B_W), lambda bi, r: (bi, 0, q_blk + 1)),
            pl.BlockSpec((None, s, B_W), lambda bi, r: (bi, 0, q_blk + 2)),
            pl.BlockSpec((N_NA_HEADS, None, GRID_W, nkeys), bias_idx),
        ],
        out_specs=pl.BlockSpec((None, GRID_W, B_W), lambda bi, r: (bi, r, 0)),
        out_shape=jax.ShapeDtypeStruct((b, s, B_W), BF16),
        compiler_params=_cparams(("arbitrary", "arbitrary"), 40),
        name="na_attn",
    )(qkv3, qkv3, qkv3, bias_tab)


def _na_bias_table(rpb, rows, kr):
    d = np.arange(kr)[:, None]
    i = np.arange(kr)[None, :]
    ridx = i - d + (NA_KR_MAX - 1)
    w = np.arange(GRID_W)[:, None]
    c = np.arange(GRID_W)[None, :]
    cs = np.clip(w - NA_KC // 2, 0, GRID_W - NA_KC)
    valid = (c >= cs) & (c < cs + NA_KC)
    cidx = np.clip(c - w + (NA_KC - 1), 0, 2 * NA_KC - 2)
    tab = rpb[:, ridx[:, :, None, None], cidx[None, None, :, :]]
    tab = jnp.where(valid[None, None, None], tab.astype(F32), NEG_BIG)
    tab = tab.transpose(0, 1, 3, 2, 4)
    return tab.reshape(rpb.shape[0], kr, GRID_W, kr * GRID_W)


def _merge_kernel(ya_ref, yb_ref, gate_ref, x_ref, wpa_ref, wpb_ref, wout_ref, g_ref, wq_ref,
                  keys_ref, h_ref, xn_ref, st_ref):
    ma = jnp.dot(ya_ref[...], wpa_ref[...], preferred_element_type=F32)
    mb = jnp.dot(yb_ref[...], wpb_ref[...], preferred_element_type=F32)
    ga = gate_ref[:, :D_MODEL]
    gb = gate_ref[:, D_MODEL:]
    merged = jax.nn.sigmoid(ga) * ma + jax.nn.sigmoid(gb) * mb
    h = x_ref[...] + jnp.dot(merged.astype(BF16), wout_ref[...], preferred_element_type=F32)
    h_ref[...] = h
    ms = jnp.mean(h * h, axis=-1, keepdims=True)
    xn = (h * lax.rsqrt(ms + EPS) * g_ref[...]).astype(BF16)
    xn_ref[...] = xn
    q = jnp.dot(xn, wq_ref[...], preferred_element_type=F32).astype(BF16)
    nt = (((1,), (1,)), ((), ()))
    half = PEER_QUERY_DIM // 2
    for hp in range(2 * PEER_HEADS):
        st_ref[hp] = lax.dot_general(keys_ref[hp], q[:, hp * half:(hp + 1) * half], nt,
                                     preferred_element_type=F32)


def _merge(ya, yb, gates, x2, wpa, wpb, wout, g_ffn, wq, keys):
    t = x2.shape[0]
    tm = TM_MERGE
    qw = wq.shape[1]
    nlist = keys.shape[0]
    full = lambda shape: pl.BlockSpec(shape, lambda i: (0,) * len(shape))
    return pl.pallas_call(
        _merge_kernel,
        grid=(t // tm,),
        in_specs=[
            pl.BlockSpec((tm, A_V_W), lambda i: (i, 0)),
            pl.BlockSpec((tm, B_W), lambda i: (i, 0)),
            pl.BlockSpec((tm, GATE_W), lambda i: (i, 0)),
            pl.BlockSpec((tm, D_MODEL), lambda i: (i, 0)),
            full((A_V_W, D_MODEL)), full((B_W, D_MODEL)), full((D_MODEL, D_MODEL)),
            full((1, D_MODEL)), full((D_MODEL, qw)), full(keys.shape),
        ],
        out_specs=[
            pl.BlockSpec((tm, D_MODEL), lambda i: (i, 0)),
            pl.BlockSpec((tm, D_MODEL), lambda i: (i, 0)),
            pl.BlockSpec((nlist, PEER_N_KEYS, tm), lambda i: (0, 0, i)),
        ],
        out_shape=[
            jax.ShapeDtypeStruct((t, D_MODEL), F32),
            jax.ShapeDtypeStruct((t, D_MODEL), BF16),
            jax.ShapeDtypeStruct((nlist, PEER_N_KEYS, t), F32),
        ],
        compiler_params=_cparams(("arbitrary",), 56),
        name="merge_query",
    )(ya, yb, gates, x2, wpa, wpb, wout, g_ffn, wq, keys)


def _staircase_layout():
    k = PEER_TOPK
    blocks = []
    for r1 in range(8):
        blocks.append((r1, 16 if r1 == 0 else 8, k // (r1 + 1)))
    pos, valid = [], []
    for r1, rows, width in blocks:
        for r2 in range(rows):
            pos.append(r1 * k + r2)
            valid.append(r2 < width)
    for r1 in range(8, k):
        pos.append(r1 * k)
        valid.append(True)
    return blocks, np.asarray(pos, np.int32), np.asarray(valid, bool)


def _extract_step(it, cur, rank, srt, key_iota, row_iota):
    m = jnp.max(cur, axis=0, keepdims=True)
    first = jnp.min(jnp.where(cur == m, key_iota, PEER_N_KEYS), axis=0, keepdims=True)
    sel = key_iota == first
    itf = it.astype(F32)
    rank = jnp.where(sel, itf, rank)
    cur = jnp.where(sel, -jnp.inf, cur)
    srt = jnp.where(row_iota == it, m, srt)
    return cur, rank, srt


def _select_kernel(pos_ref, valid_ref, s_ref, cnt_ref, ea_ref, rank2_ref, eb_ref):
    k = PEER_TOPK
    a = s_ref[0]
    b = s_ref[1]
    nk, nt = a.shape
    key_iota = lax.broadcasted_iota(jnp.int32, (nk, nt), 0)
    row_iota = lax.broadcasted_iota(jnp.int32, (k, nt), 0)
    unranked = jnp.full((nk, nt), float(nk), F32)
    srt0 = jnp.zeros((k, nt), F32)

    def body(it, carry):
        ca, ra, sa, cb, rb, sb = carry
        ca, ra, sa = _extract_step(it, ca, ra, sa, key_iota, row_iota)
        cb, rb, sb = _extract_step(it, cb, rb, sb, key_iota, row_iota)
        return ca, ra, sa, cb, rb, sb

    _, rank_a, as_, _, rank_b, bs_ = lax.fori_loop(
        0, k, body, (a, unranked, srt0, b, unranked, srt0))

    blocks, _, _ = _staircase_layout()
    parts = [as_[r1:r1 + 1] + bs_[0:rows] for r1, rows, _ in blocks]
    parts.append(as_[8:k] + bs_[0:1])
    cand0 = jnp.concatenate(parts, axis=0)
    pos = pos_ref[...]
    cand0 = jnp.where(valid_ref[...] > 0, cand0, -jnp.inf)
    far = k * k

    def pick(_, carry):
        cand, taken = carry
        m = jnp.max(cand, axis=0, keepdims=True)
        first = jnp.min(jnp.where(cand == m, pos, far), axis=0, keepdims=True)
        sel = pos == first
        return jnp.where(sel, -jnp.inf, cand), jnp.where(sel, 1.0, taken)

    _, taken = lax.fori_loop(0, k, pick, (cand0, jnp.zeros_like(cand0)))

    top = as_[0:1] + bs_[0:1]
    z = jnp.sum(taken * jnp.exp(cand0 - top), axis=0, keepdims=True)

    counts = []
    off = 0
    for _, rows, _ in blocks:
        counts.append(jnp.sum(taken[off:off + rows], axis=0, keepdims=True))
        off += rows
    for j in range(k - 8):
        counts.append(taken[off + j:off + j + 1])
    cnt = jnp.zeros((nk, nt), F32)
    for r1 in range(k):
        cnt = jnp.where(rank_a == float(r1), counts[r1], cnt)

    cnt_ref[...] = cnt
    ea_ref[...] = jnp.exp(a - as_[0:1])
    rank2_ref[...] = rank_b
    eb_ref[...] = jnp.exp(b - bs_[0:1]) * (1.0 / z)


def _select(scores_t):
    nlist, nk, t = scores_t.shape
    heads = nlist // 2
    _, pos, valid = _staircase_layout()
    ncand = pos.shape[0]
    pos_arr = jnp.asarray(np.broadcast_to(pos[:, None], (ncand, LANES)).copy())
    valid_arr = jnp.asarray(np.broadcast_to(valid[:, None].astype(np.int32), (ncand, LANES)).copy())
    tab = pl.BlockSpec((None, nk, LANES), lambda i, h: (h, 0, i))
    const = pl.BlockSpec((ncand, LANES), lambda i, h: (0, 0))
    shape = jax.ShapeDtypeStruct((heads, nk, t), F32)
    return pl.pallas_call(
        _select_kernel,
        grid=(t // LANES, heads),
        in_specs=[const, const, pl.BlockSpec((2, nk, LANES), lambda i, h: (h, 0, i))],
        out_specs=[tab, tab, tab, tab],
        out_shape=[shape, shape, shape, shape],
        compiler_params=_cparams(("arbitrary", "arbitrary"), 32),
        name="peer_select",
    )(pos_arr, valid_arr, scores_t)


def _peer_mix_kernel(final_norm, xn_ref, u_ref, vt_ref, cnt_ref, ea_ref, rank2_ref, eb_ref,
                     h_ref, g_ref, o_ref, at_ref, p_ref, acc_ref):
    e = pl.program_id(1)
    eb, tt = at_ref.shape
    rows_per_step = eb // PEER_N_KEYS
    nt = (((1,), (1,)), ((), ()))

    @pl.when(e == 0)
    def _():
        acc_ref[...] = jnp.zeros_like(acc_ref)

    at_ref[...] = lax.dot_general(u_ref[...], xn_ref[...], nt, preferred_element_type=F32)

    def lane_block(tc, carry):
        cols = pl.ds(pl.multiple_of(tc * LANES, LANES), LANES)
        for j in range(rows_per_step):
            rows = slice(j * PEER_N_KEYS, (j + 1) * PEER_N_KEYS)
            w = jnp.zeros((PEER_N_KEYS, LANES), F32)
            for h in range(PEER_HEADS):
                cnt_row = cnt_ref[h, j:j + 1, cols]
                ea_row = ea_ref[h, j:j + 1, cols]
                w = w + jnp.where(rank2_ref[h, :, cols] < cnt_row, eb_ref[h, :, cols], 0.0) * ea_row
            act = at_ref[rows, cols]
            gelu = 0.5 * act * (1.0 + lax.erf(act * math.sqrt(0.5)))
            p_ref[rows, cols] = (w * gelu).astype(BF16)
        return carry

    lax.fori_loop(0, tt // LANES, lane_block, 0)
    acc_ref[...] += jnp.dot(vt_ref[...], p_ref[...], preferred_element_type=F32)

    @pl.when(e == pl.num_programs(1) - 1)
    def _():
        y = h_ref[...] + acc_ref[...].T
        if final_norm:
            ms = jnp.mean(y * y, axis=-1, keepdims=True)
            y = y * lax.rsqrt(ms + EPS) * g_ref[...]
        o_ref[...] = y


def _peer_mix(xn, u_bf, vt_bf, cnt, ea, rank2, eb, h, g_final, final_norm):
    t = xn.shape[0]
    tt, ebk = TT_PEER, EB_PEER
    n_exp = u_bf.shape[0]
    heads, nk, _ = cnt.shape
    tab = pl.BlockSpec((heads, nk, tt), lambda ti, e: (0, 0, ti))
    row_tab = pl.BlockSpec((heads, ebk // nk, tt), lambda ti, e: (0, e, ti))
    return pl.pallas_call(
        functools.partial(_peer_mix_kernel, final_norm),
        grid=(t // tt, n_exp // ebk),
        in_specs=[
            pl.BlockSpec((tt, D_MODEL), lambda ti, e: (ti, 0)),
            pl.BlockSpec((ebk, D_MODEL), lambda ti, e: (e, 0)),
            pl.BlockSpec((D_MODEL, ebk), lambda ti, e: (0, e)),
            row_tab, row_tab, tab, tab,
            pl.BlockSpec((tt, D_MODEL), lambda ti, e: (ti, 0)),
            pl.BlockSpec((1, D_MODEL), lambda ti, e: (0, 0)),
        ],
        out_specs=pl.BlockSpec((tt, D_MODEL), lambda ti, e: (ti, 0)),
        out_shape=jax.ShapeDtypeStruct((t, D_MODEL), F32),
        scratch_shapes=[
            pltpu.VMEM((ebk, tt), F32),
            pltpu.VMEM((ebk, tt), BF16),
            pltpu.VMEM((D_MODEL, tt), F32),
        ],
        compiler_params=_cparams(("arbitrary", "arbitrary"), 56),
        name="peer_mix",
    )(xn, u_bf, vt_bf, cnt, ea, rank2, eb, h, g_final)


def _rope_tables(seq):
    half = ROPE_DIM // 2
    inv_freq = ROPE_THETA ** (-jnp.arange(0, ROPE_DIM, 2, dtype=F32) / ROPE_DIM)
    ang = jnp.arange(seq, dtype=F32)[:, None] * inv_freq[None, :]
    cos, sin = jnp.cos(ang), jnp.sin(ang)
    pad = jnp.zeros((seq, DIFF_HEAD_DIM - ROPE_DIM), F32)
    c64 = jnp.concatenate([cos, cos, pad + 1.0], axis=1)
    up64 = jnp.concatenate([jnp.zeros_like(sin), sin, pad], axis=1)
    dn64 = jnp.concatenate([-sin, jnp.zeros_like(sin), pad], axis=1)
    rep = LANES // DIFF_HEAD_DIM
    return jnp.tile(c64, (1, rep)), jnp.tile(up64, (1, rep)), jnp.tile(dn64, (1, rep))


def kernel(x, w_in, w_proj_a, w_proj_b, w_out, norm_mix, norm_ffn, norm_final, lambda_q1,
           lambda_k1, lambda_q2, lambda_k2, diff_subln, na_rpb, peer_w_query, peer_sub_keys,
           peer_u, peer_v):
    b, s, d = x.shape
    t = b * s
    depth = w_in.shape[0]
    rows = s // GRID_W
    kr = min(NA_KR_MAX, rows)
    cos_t, sin_up_t, sin_dn_t = _rope_tables(s)
    h = x.reshape(t, d)
    row = lambda v: v.reshape(1, -1).astype(F32)
    for l in range(depth):
        lam_init = 0.8 - 0.6 * math.exp(-0.3 * l)
        qkv, gates = _inproj(h, row(norm_mix[l]), w_in[l].astype(BF16), cos_t, sin_up_t,
                             sin_dn_t, s)
        qkv3 = qkv.reshape(b, s, QKV_W)
        ya = _diff_attn(qkv3, row(lambda_q1[l]), row(lambda_k1[l]), row(lambda_q2[l]),
                        row(lambda_k2[l]), row(diff_subln[l]), lam_init)
        yb = _na_attn(qkv3, _na_bias_table(na_rpb[l], rows, kr), rows, kr)
        keys = peer_sub_keys[l].reshape(2 * PEER_HEADS, PEER_N_KEYS, PEER_QUERY_DIM // 2)
        h_mid, xn, scores_t = _merge(
            ya.reshape(t, A_V_W), yb.reshape(t, B_W), gates, h,
            w_proj_a[l].astype(BF16), w_proj_b[l].astype(BF16), w_out[l].astype(BF16),
            row(norm_ffn[l]), peer_w_query[l].astype(BF16), keys.astype(BF16))
        cnt, ea, rank2, eb = _select(scores_t)
        h = _peer_mix(xn, peer_u[l].astype(BF16), peer_v[l].T.astype(BF16), cnt, ea, rank2, eb,
                      h_mid, row(norm_final), final_norm=(l == depth - 1))
    return h.reshape(b, s, d)
```

```python
import functools
import math

import jax
import jax.numpy as jnp
import numpy as np
from jax import lax
from jax.experimental import pallas as pl
from jax.experimental.pallas import tpu as pltpu

F32 = jnp.float32
BF16 = jnp.bfloat16

D_MODEL = 1024
GRID_W = 64
N_DIFF_HEADS = 4
DIFF_HEAD_DIM = 64
DIFF_V_DIM = 2 * DIFF_HEAD_DIM
ROPE_THETA = 500000.0
ROPE_DIM = DIFF_HEAD_DIM // 4
N_NA_HEADS = 8
NA_HEAD_DIM = 64
NA_KR_MAX = 8
NA_KC = 16
A_QK_W = N_DIFF_HEADS * 2 * DIFF_HEAD_DIM
A_V_W = N_DIFF_HEADS * DIFF_V_DIM
B_W = N_NA_HEADS * NA_HEAD_DIM
QKV_W = 2 * A_QK_W + A_V_W + 3 * B_W
GATE_W = 2 * D_MODEL
PEER_HEADS = 8
PEER_N_KEYS = 128
PEER_N_EXPERTS = PEER_N_KEYS * PEER_N_KEYS
PEER_QUERY_DIM = 256
PEER_TOPK = 16
EPS = 1e-6

LANES = 128
NEG_BIG = -1e30
MIB = 1024 * 1024

TM_INPROJ = 512
TQ_DIFF = 256
TM_MERGE = 256
TT_PEER = 512
EB_PEER = 1024
SB_PEER = 512


def _cparams(sem, vmem_mib, flags=None):
    return pltpu.CompilerParams(dimension_semantics=sem, vmem_limit_bytes=vmem_mib * MIB,
                                flags=flags)


def _inproj_kernel(x_ref, g_ref, w_ref, c_ref, sa_ref, sb_ref, qkv_ref, gate_ref):
    x = x_ref[...]
    ms = jnp.mean(x * x, axis=-1, keepdims=True)
    n = (x * lax.rsqrt(ms + EPS) * g_ref[...]).astype(BF16)
    chunk = 512
    reps = chunk // LANES
    cos = jnp.tile(c_ref[...], (1, reps))
    sin_up = jnp.tile(sa_ref[...], (1, reps))
    sin_dn = jnp.tile(sb_ref[...], (1, reps))
    half = ROPE_DIM // 2
    for j in range(QKV_W // chunk):
        y = jnp.dot(n, w_ref[:, j * chunk:(j + 1) * chunk], preferred_element_type=F32)
        if j < 2 * A_QK_W // chunk:
            y = (y * cos + pltpu.roll(y, half, axis=1) * sin_up
                 + pltpu.roll(y, chunk - half, axis=1) * sin_dn)
            if j < A_QK_W // chunk:
                y = y * (DIFF_HEAD_DIM ** -0.5)
        qkv_ref[:, j * chunk:(j + 1) * chunk] = y.astype(BF16)
    for j in range(GATE_W // chunk):
        c0 = QKV_W + j * chunk
        gate_ref[:, j * chunk:(j + 1) * chunk] = jnp.dot(
            n, w_ref[:, c0:c0 + chunk], preferred_element_type=F32)


def _inproj(x2, g, w_bf, cos_t, sin_up_t, sin_dn_t, seq):
    t = x2.shape[0]
    tm = TM_INPROJ
    nblk = seq // tm
    in_w = w_bf.shape[1]
    tab_spec = pl.BlockSpec((tm, LANES), lambda i: (i % nblk, 0))
    return pl.pallas_call(
        _inproj_kernel,
        grid=(t // tm,),
        in_specs=[
            pl.BlockSpec((tm, D_MODEL), lambda i: (i, 0)),
            pl.BlockSpec((1, D_MODEL), lambda i: (0, 0)),
            pl.BlockSpec((D_MODEL, in_w), lambda i: (0, 0)),
            tab_spec, tab_spec, tab_spec,
        ],
        out_specs=[
            pl.BlockSpec((tm, QKV_W), lambda i: (i, 0)),
            pl.BlockSpec((tm, GATE_W), lambda i: (i, 0)),
        ],
        out_shape=[
            jax.ShapeDtypeStruct((t, QKV_W), BF16),
            jax.ShapeDtypeStruct((t, GATE_W), F32),
        ],
        compiler_params=_cparams(("arbitrary",), 56),
        name="inproj",
    )(x2, g, w_bf, cos_t, sin_up_t, sin_dn_t)


def _diff_attn_kernel(lam_init, q_ref, k_ref, v_ref, lq1_ref, lk1_ref, lq2_ref, lk2_ref,
                      g_ref, o_ref):
    q = q_ref[...]
    k = k_ref[...]
    lane = lax.broadcasted_iota(jnp.int32, q.shape, 1)
    zero = jnp.zeros_like(q)
    q1 = jnp.where(lane < DIFF_HEAD_DIM, q, zero)
    q2 = jnp.where(lane >= DIFF_HEAD_DIM, q, zero)
    nt = (((1,), (1,)), ((), ()))
    s1 = lax.dot_general(q1, k, nt, preferred_element_type=F32)
    s2 = lax.dot_general(q2, k, nt, preferred_element_type=F32)
    e1 = jnp.exp(s1 - jnp.max(s1, axis=-1, keepdims=True))
    e2 = jnp.exp(s2 - jnp.max(s2, axis=-1, keepdims=True))
    l1 = jnp.sum(e1, axis=-1, keepdims=True)
    l2 = jnp.sum(e2, axis=-1, keepdims=True)
    lam = (jnp.exp(jnp.sum(lq1_ref[...] * lk1_ref[...], axis=-1, keepdims=True))
           - jnp.exp(jnp.sum(lq2_ref[...] * lk2_ref[...], axis=-1, keepdims=True))
           + lam_init)
    a = e1 * (1.0 / l1) - e2 * (lam / l2)
    o = jnp.dot(a.astype(BF16), v_ref[...], preferred_element_type=F32)
    ms = jnp.mean(o * o, axis=-1, keepdims=True)
    o = o * lax.rsqrt(ms + EPS) * g_ref[...] * (1.0 - lam_init)
    o_ref[...] = o.astype(o_ref.dtype)


def _diff_attn(qkv3, lq1, lk1, lq2, lk2, subln_g, lam_init):
    b, s, _ = qkv3.shape
    tq = TQ_DIFF
    hw = 2 * DIFF_HEAD_DIM
    k_blk0 = A_QK_W // hw
    v_blk0 = 2 * A_QK_W // hw
    vec = pl.BlockSpec((1, DIFF_HEAD_DIM), lambda bi, h, i: (0, 0))
    return pl.pallas_call(
        functools.partial(_diff_attn_kernel, lam_init),
        grid=(b, N_DIFF_HEADS, s // tq),
        in_specs=[
            pl.BlockSpec((None, tq, hw), lambda bi, h, i: (bi, i, h)),
            pl.BlockSpec((None, s, hw), lambda bi, h, i: (bi, 0, k_blk0 + h)),
            pl.BlockSpec((None, s, DIFF_V_DIM), lambda bi, h, i: (bi, 0, v_blk0 + h)),
            vec, vec, vec, vec,
            pl.BlockSpec((1, DIFF_V_DIM), lambda bi, h, i: (0, 0)),
        ],
        out_specs=pl.BlockSpec((None, tq, DIFF_V_DIM), lambda bi, h, i: (bi, i, h)),
        out_shape=jax.ShapeDtypeStruct((b, s, A_V_W), BF16),
        compiler_params=_cparams(("arbitrary", "arbitrary", "arbitrary"), 48),
        name="diff_attn",
    )(qkv3, qkv3, qkv3, lq1, lk1, lq2, lk2, subln_g)


def _na_window_start(r, rows, kr):
    return jnp.clip(r - kr // 2, 0, rows - kr)


def _na_kernel(rows, kr, q_ref, k_ref, v_ref, bias_ref, o_ref):
    r = pl.program_id(1)
    start = pl.multiple_of(_na_window_start(r, rows, kr) * GRID_W, GRID_W)
    nkeys = kr * GRID_W
    kwin = k_ref[pl.ds(start, nkeys), :]
    vwin = v_ref[pl.ds(start, nkeys), :]
    q = q_ref[...] * (NA_HEAD_DIM ** -0.5)
    nt = (((1,), (1,)), ((), ()))
    pair_w = 2 * NA_HEAD_DIM
    lane = lax.broadcasted_iota(jnp.int32, (GRID_W, pair_w), 1)
    for hp in range(N_NA_HEADS // 2):
        sl = slice(hp * pair_w, (hp + 1) * pair_w)
        qp, kp, vp = q[:, sl], kwin[:, sl], vwin[:, sl]
        outs = []
        for hh in range(2):
            keep = (lane < NA_HEAD_DIM) if hh == 0 else (lane >= NA_HEAD_DIM)
            qm = jnp.where(keep, qp, jnp.zeros_like(qp))
            s = lax.dot_general(qm, kp, nt, preferred_element_type=F32)
            s = s + bias_ref[2 * hp + hh]
            e = jnp.exp(s - jnp.max(s, axis=-1, keepdims=True))
            p = e * (1.0 / jnp.sum(e, axis=-1, keepdims=True))
            outs.append(jnp.dot(p.astype(BF16), vp, preferred_element_type=F32))
        o_ref[:, sl] = jnp.where(lane < NA_HEAD_DIM, outs[0], outs[1]).astype(o_ref.dtype)


def _na_attn(qkv3, bias_tab, rows, kr):
    b, s, _ = qkv3.shape
    q_blk = (2 * A_QK_W + A_V_W) // B_W
    nkeys = kr * GRID_W

    def bias_idx(bi, r):
        return (0, r - _na_window_start(r, rows, kr), 0, 0)

    return pl.pallas_call(
        functools.partial(_na_kernel, rows, kr),
        grid=(b, rows),
        in_specs=[
            pl.BlockSpec((None, GRID_W, B_W), lambda bi, r: (bi, r, q_blk)),
            pl.BlockSpec((None, s, B_W), lambda bi, r: (bi, 0, q_blk + 1)),
            pl.BlockSpec((None, s, B_W), lambda bi, r: (bi, 0, q_blk + 2)),
            pl.BlockSpec((N_NA_HEADS, None, GRID_W, nkeys), bias_idx),
        ],
        out_specs=pl.BlockSpec((None, GRID_W, B_W), lambda bi, r: (bi, r, 0)),
        out_shape=jax.ShapeDtypeStruct((b, s, B_W), BF16),
        compiler_params=_cparams(("arbitrary", "arbitrary"), 40),
        name="na_attn",
    )(qkv3, qkv3, qkv3, bias_tab)


def _na_bias_table(rpb, rows, kr):
    w = np.arange(GRID_W)[:, None]
    c = np.arange(GRID_W)[None, :]
    cs = np.clip(w - NA_KC // 2, 0, GRID_W - NA_KC)
    valid = (c >= cs) & (c < cs + NA_KC)
    ncol = 2 * NA_KC - 1
    onehot = ((c - w + (NA_KC - 1))[None] == np.arange(ncol)[:, None, None]) & valid[None]
    top = NA_KR_MAX - 1
    rsel = jnp.stack([rpb[:, top - d:top - d + kr, :] for d in range(kr)], axis=1).astype(F32)
    tab = jnp.einsum('hdik,kwc->hdwic', rsel, jnp.asarray(onehot, F32),
                     precision=lax.Precision.HIGHEST)
    tab = jnp.where(valid[None, None, :, None, :], tab, NEG_BIG)
    return tab.reshape(rpb.shape[0], kr, GRID_W, kr * GRID_W)


def _merge_kernel(ya_ref, yb_ref, gate_ref, x_ref, wpa_ref, wpb_ref, wout_ref, g_ref, wq_ref,
                  keys_ref, h_ref, xnt_ref, st_ref):
    ma = jnp.dot(ya_ref[...], wpa_ref[...], preferred_element_type=F32)
    mb = jnp.dot(yb_ref[...], wpb_ref[...], preferred_element_type=F32)
    ga = gate_ref[:, :D_MODEL]
    gb = gate_ref[:, D_MODEL:]
    merged = jax.nn.sigmoid(ga) * ma + jax.nn.sigmoid(gb) * mb
    h = x_ref[...] + jnp.dot(merged.astype(BF16), wout_ref[...], preferred_element_type=F32)
    h_ref[...] = h
    ms = jnp.mean(h * h, axis=-1, keepdims=True)
    xn_f32 = h * lax.rsqrt(ms + EPS) * g_ref[...]
    xn = xn_f32.astype(BF16)
    xnt_ref[...] = xn_f32.T.astype(BF16)
    q = jnp.dot(xn, wq_ref[...], preferred_element_type=F32).astype(BF16)
    nt = (((1,), (1,)), ((), ()))
    half = PEER_QUERY_DIM // 2
    for hp in range(2 * PEER_HEADS):
        st_ref[hp] = lax.dot_general(keys_ref[hp], q[:, hp * half:(hp + 1) * half], nt,
                                     preferred_element_type=F32)


def _merge(ya, yb, gates, x2, wpa, wpb, wout, g_ffn, wq, keys):
    t = x2.shape[0]
    tm = TM_MERGE
    qw = wq.shape[1]
    nlist = keys.shape[0]
    full = lambda shape: pl.BlockSpec(shape, lambda i: (0,) * len(shape))
    return pl.pallas_call(
        _merge_kernel,
        grid=(t // tm,),
        in_specs=[
            pl.BlockSpec((tm, A_V_W), lambda i: (i, 0)),
            pl.BlockSpec((tm, B_W), lambda i: (i, 0)),
            pl.BlockSpec((tm, GATE_W), lambda i: (i, 0)),
            pl.BlockSpec((tm, D_MODEL), lambda i: (i, 0)),
            full((A_V_W, D_MODEL)), full((B_W, D_MODEL)), full((D_MODEL, D_MODEL)),
            full((1, D_MODEL)), full((D_MODEL, qw)), full(keys.shape),
        ],
        out_specs=[
            pl.BlockSpec((tm, D_MODEL), lambda i: (i, 0)),
            pl.BlockSpec((D_MODEL, tm), lambda i: (0, i)),
            pl.BlockSpec((nlist, PEER_N_KEYS, tm), lambda i: (0, 0, i)),
        ],
        out_shape=[
            jax.ShapeDtypeStruct((t, D_MODEL), F32),
            jax.ShapeDtypeStruct((D_MODEL, t), BF16),
            jax.ShapeDtypeStruct((nlist, PEER_N_KEYS, t), F32),
        ],
        compiler_params=_cparams(("arbitrary",), 56),
        name="merge_query",
    )(ya, yb, gates, x2, wpa, wpb, wout, g_ffn, wq, keys)


def _staircase_layout():
    k = PEER_TOPK
    blocks = []
    for r1 in range(8):
        blocks.append((r1, 16 if r1 == 0 else 8, k // (r1 + 1)))
    pos, valid = [], []
    for r1, rows, width in blocks:
        for r2 in range(rows):
            pos.append(r1 * k + r2)
            valid.append(r2 < width)
    for r1 in range(8, k):
        pos.append(r1 * k)
        valid.append(True)
    return blocks, np.asarray(pos, np.int32), np.asarray(valid, bool)


def _extract_step(it, cur, rank, srt, key_iota, row_iota):
    m = jnp.max(cur, axis=0, keepdims=True)
    first = jnp.min(jnp.where(cur == m, key_iota, PEER_N_KEYS), axis=0, keepdims=True)
    sel = key_iota == first
    itf = it.astype(F32)
    rank = jnp.where(sel, itf, rank)
    cur = jnp.where(sel, -jnp.inf, cur)
    srt = jnp.where(row_iota == it, m, srt)
    return cur, rank, srt


def _dup_bf16(x):
    bits = lax.bitcast_convert_type(x.astype(BF16).astype(F32), jnp.uint32)
    return bits | (bits >> 16)


def _select_kernel(pos_ref, valid_ref, s_ref, cnt_ref, ea_ref, rank2_ref, eb_ref):
    k = PEER_TOPK
    a = s_ref[0]
    b = s_ref[1]
    nk, nt = a.shape
    key_iota = lax.broadcasted_iota(jnp.int32, (nk, nt), 0)
    row_iota = lax.broadcasted_iota(jnp.int32, (k, nt), 0)
    unranked = jnp.full((nk, nt), float(nk), F32)
    srt0 = jnp.zeros((k, nt), F32)

    def body(it, carry):
        ca, ra, sa, cb, rb, sb = carry
        ca, ra, sa = _extract_step(it, ca, ra, sa, key_iota, row_iota)
        cb, rb, sb = _extract_step(it, cb, rb, sb, key_iota, row_iota)
        return ca, ra, sa, cb, rb, sb

    _, rank_a, as_, _, rank_b, bs_ = lax.fori_loop(
        0, k, body, (a, unranked, srt0, b, unranked, srt0))

    blocks, _, _ = _staircase_layout()
    parts = [as_[r1:r1 + 1] + bs_[0:rows] for r1, rows, _ in blocks]
    parts.append(as_[8:k] + bs_[0:1])
    cand0 = jnp.concatenate(parts, axis=0)
    pos = pos_ref[...]
    cand0 = jnp.where(valid_ref[...] > 0, cand0, -jnp.inf)
    far = k * k

    def pick(_, carry):
        cand, taken = carry
        m = jnp.max(cand, axis=0, keepdims=True)
        first = jnp.min(jnp.where(cand == m, pos, far), axis=0, keepdims=True)
        sel = pos == first
        return jnp.where(sel, -jnp.inf, cand), jnp.where(sel, 1.0, taken)

    _, taken = lax.fori_loop(0, k, pick, (cand0, jnp.zeros_like(cand0)))

    top = as_[0:1] + bs_[0:1]
    z = jnp.sum(taken * jnp.exp(cand0 - top), axis=0, keepdims=True)

    counts = []
    off = 0
    for _, rows, _ in blocks:
        counts.append(jnp.sum(taken[off:off + rows], axis=0, keepdims=True))
        off += rows
    for j in range(k - 8):
        counts.append(taken[off + j:off + j + 1])
    cnt = jnp.zeros((nk, nt), F32)
    for r1 in range(k):
        cnt = jnp.where(rank_a == float(r1), counts[r1], cnt)

    cnt_ref[...] = _dup_bf16(cnt)
    ea_ref[...] = _dup_bf16(jnp.exp(a - as_[0:1]))
    rank2_ref[...] = rank_b
    eb_ref[...] = jnp.exp(b - bs_[0:1]) * (1.0 / z)


def _select(scores_t):
    nlist, nk, t = scores_t.shape
    heads = nlist // 2
    _, pos, valid = _staircase_layout()
    ncand = pos.shape[0]
    pos_arr = jnp.asarray(np.broadcast_to(pos[:, None], (ncand, LANES)).copy())
    valid_arr = jnp.asarray(np.broadcast_to(valid[:, None].astype(np.int32), (ncand, LANES)).copy())
    tab = pl.BlockSpec((None, nk, LANES), lambda i, h: (h, 0, i))
    const = pl.BlockSpec((ncand, LANES), lambda i, h: (0, 0))
    dup = jax.ShapeDtypeStruct((heads, nk, t), jnp.uint32)
    half = jax.ShapeDtypeStruct((heads, nk, t), F32)
    return pl.pallas_call(
        _select_kernel,
        grid=(t // LANES, heads),
        in_specs=[const, const, pl.BlockSpec((2, nk, LANES), lambda i, h: (h, 0, i))],
        out_specs=[tab, tab, tab, tab],
        out_shape=[dup, dup, half, half],
        compiler_params=_cparams(("arbitrary", "arbitrary"), 32),
        name="peer_select",
    )(pos_arr, valid_arr, scores_t)


def _peer_gate_tiles(j_rows, lane_chunks, cnt_ref, ea_ref, r2_ref, ebh_ref, at_ref, p_ref):
    for j in j_rows:
        rows = slice(j * PEER_N_KEYS, (j + 1) * PEER_N_KEYS)
        for tc in lane_chunks:
            cols = slice(tc * LANES, (tc + 1) * LANES)
            w = None
            for h in range(PEER_HEADS):
                bcast = lambda ref: pltpu.bitcast(
                    jnp.broadcast_to(ref[h, j:j + 1, cols], (PEER_N_KEYS // 2, LANES)), BF16)
                term = jnp.where(r2_ref[h, tc] < bcast(cnt_ref), ebh_ref[h, tc], 0.0) * bcast(ea_ref)
                w = term if w is None else w + term
            act = at_ref[rows, cols]
            gelu = 0.5 * act * (1.0 + lax.erf(act * math.sqrt(0.5)))
            p_ref[rows, cols] = w * gelu.astype(BF16)


def _peer_mix_kernel(final_norm, xnt_ref, u_ref, vt_ref, cnt_ref, ea_ref, rank2_ref, eb_ref,
                     h_ref, g_ref, o_ref, at_ref, p_ref, acc_ref, r2_ref, ebh_ref):
    e = pl.program_id(1)
    eb, tt = at_ref.shape

    @pl.when(e == 0)
    def _():
        acc_ref[...] = jnp.zeros_like(acc_ref)
        for h in range(PEER_HEADS):
            for tc in range(tt // LANES):
                cols = slice(tc * LANES, (tc + 1) * LANES)
                r2_ref[h, tc] = rank2_ref[h, :, cols].astype(BF16)
                ebh_ref[h, tc] = eb_ref[h, :, cols].astype(BF16)

    rows_per_sub = SB_PEER // PEER_N_KEYS
    for sb in range(eb // SB_PEER):
        sub = slice(sb * SB_PEER, (sb + 1) * SB_PEER)
        at_ref[sub, :] = jnp.dot(u_ref[sub, :], xnt_ref[...], preferred_element_type=F32)
        _peer_gate_tiles(range(sb * rows_per_sub, (sb + 1) * rows_per_sub), range(tt // LANES),
                         cnt_ref, ea_ref, r2_ref, ebh_ref, at_ref, p_ref)
    acc_ref[...] += jnp.dot(vt_ref[...], p_ref[...], preferred_element_type=F32)

    @pl.when(e == pl.num_programs(1) - 1)
    def _():
        y = h_ref[...] + acc_ref[...].T
        if final_norm:
            ms = jnp.mean(y * y, axis=-1, keepdims=True)
            y = y * lax.rsqrt(ms + EPS) * g_ref[...]
        o_ref[...] = y


def _peer_mix(xnt, u_bf, vt_bf, cnt, ea, rank2, eb, h, g_final, final_norm):
    t = xnt.shape[1]
    tt, ebk = TT_PEER, EB_PEER
    n_exp = u_bf.shape[0]
    heads, nk, _ = cnt.shape
    tab = pl.BlockSpec((heads, nk, tt), lambda ti, e: (0, 0, ti))
    row_tab = pl.BlockSpec((heads, ebk // nk, tt), lambda ti, e: (0, e, ti))
    return pl.pallas_call(
        functools.partial(_peer_mix_kernel, final_norm),
        grid=(t // tt, n_exp // ebk),
        in_specs=[
            pl.BlockSpec((D_MODEL, tt), lambda ti, e: (0, ti)),
            pl.BlockSpec((ebk, D_MODEL), lambda ti, e: (e, 0)),
            pl.BlockSpec((D_MODEL, ebk), lambda ti, e: (0, e)),
            row_tab, row_tab, tab, tab,
            pl.BlockSpec((tt, D_MODEL), lambda ti, e: (ti, 0)),
            pl.BlockSpec((1, D_MODEL), lambda ti, e: (0, 0)),
        ],
        out_specs=pl.BlockSpec((tt, D_MODEL), lambda ti, e: (ti, 0)),
        out_shape=jax.ShapeDtypeStruct((t, D_MODEL), F32),
        scratch_shapes=[
            pltpu.VMEM((ebk, tt), F32),
            pltpu.VMEM((ebk, tt), BF16),
            pltpu.VMEM((D_MODEL, tt), F32),
            pltpu.VMEM((heads, tt // LANES, nk, LANES), BF16),
            pltpu.VMEM((heads, tt // LANES, nk, LANES), BF16),
        ],
        compiler_params=_cparams(("arbitrary", "arbitrary"), 56),
        name="peer_mix",
    )(xnt, u_bf, vt_bf, cnt, ea, rank2, eb, h, g_final)


def _rope_tables(seq):
    half = ROPE_DIM // 2
    inv_freq = ROPE_THETA ** (-jnp.arange(0, ROPE_DIM, 2, dtype=F32) / ROPE_DIM)
    ang = jnp.arange(seq, dtype=F32)[:, None] * inv_freq[None, :]
    cos, sin = jnp.cos(ang), jnp.sin(ang)
    pad = jnp.zeros((seq, DIFF_HEAD_DIM - ROPE_DIM), F32)
    c64 = jnp.concatenate([cos, cos, pad + 1.0], axis=1)
    up64 = jnp.concatenate([jnp.zeros_like(sin), sin, pad], axis=1)
    dn64 = jnp.concatenate([-sin, jnp.zeros_like(sin), pad], axis=1)
    rep = LANES // DIFF_HEAD_DIM
    return jnp.tile(c64, (1, rep)), jnp.tile(up64, (1, rep)), jnp.tile(dn64, (1, rep))


def kernel(x, w_in, w_proj_a, w_proj_b, w_out, norm_mix, norm_ffn, norm_final, lambda_q1,
           lambda_k1, lambda_q2, lambda_k2, diff_subln, na_rpb, peer_w_query, peer_sub_keys,
           peer_u, peer_v):
    b, s, d = x.shape
    t = b * s
    depth = w_in.shape[0]
    rows = s // GRID_W
    kr = min(NA_KR_MAX, rows)
    cos_t, sin_up_t, sin_dn_t = _rope_tables(s)
    h = x.reshape(t, d)
    row = lambda v: v.reshape(1, -1).astype(F32)
    for l in range(depth):
        lam_init = 0.8 - 0.6 * math.exp(-0.3 * l)
        qkv, gates = _inproj(h, row(norm_mix[l]), w_in[l].astype(BF16), cos_t, sin_up_t,
                             sin_dn_t, s)
        qkv3 = qkv.reshape(b, s, QKV_W)
        ya = _diff_attn(qkv3, row(lambda_q1[l]), row(lambda_k1[l]), row(lambda_q2[l]),
                        row(lambda_k2[l]), row(diff_subln[l]), lam_init)
        yb = _na_attn(qkv3, _na_bias_table(na_rpb[l], rows, kr), rows, kr)
        keys = peer_sub_keys[l].reshape(2 * PEER_HEADS, PEER_N_KEYS, PEER_QUERY_DIM // 2)
        h_mid, xnt, scores_t = _merge(
            ya.reshape(t, A_V_W), yb.reshape(t, B_W), gates, h,
            w_proj_a[l].astype(BF16), w_proj_b[l].astype(BF16), w_out[l].astype(BF16),
            row(norm_ffn[l]), peer_w_query[l].astype(BF16), keys.astype(BF16))
        cnt, ea, rank2, eb = _select(scores_t)
        h = _peer_mix(xnt, peer_u[l].astype(BF16), peer_v[l].T.astype(BF16), cnt, ea, rank2, eb,
                      h_mid, row(norm_final), final_norm=(l == depth - 1))
    return h.reshape(b, s, d)
```

```python
import functools
import math

import jax
import jax.numpy as jnp
import numpy as np
from jax import lax
from jax.experimental import pallas as pl
from jax.experimental.pallas import tpu as pltpu

F32 = jnp.float32
BF16 = jnp.bfloat16

D_MODEL = 1024
GRID_W = 64
N_DIFF_HEADS = 4
DIFF_HEAD_DIM = 64
DIFF_V_DIM = 2 * DIFF_HEAD_DIM
ROPE_THETA = 500000.0
ROPE_DIM = DIFF_HEAD_DIM // 4
N_NA_HEADS = 8
NA_HEAD_DIM = 64
NA_KR_MAX = 8
NA_KC = 16
A_QK_W = N_DIFF_HEADS * 2 * DIFF_HEAD_DIM
A_V_W = N_DIFF_HEADS * DIFF_V_DIM
B_W = N_NA_HEADS * NA_HEAD_DIM
QKV_W = 2 * A_QK_W + A_V_W + 3 * B_W
GATE_W = 2 * D_MODEL
PEER_HEADS = 8
PEER_N_KEYS = 128
PEER_N_EXPERTS = PEER_N_KEYS * PEER_N_KEYS
PEER_QUERY_DIM = 256
PEER_TOPK = 16
EPS = 1e-6

LANES = 128
NEG_BIG = -1e30
MIB = 1024 * 1024

TM_INPROJ = 512
TQ_DIFF = 256
TM_MERGE = 256
TT_PEER = 512
EB_PEER = 1024
SB_PEER = 512


def _cparams(sem, vmem_mib, flags=None):
    return pltpu.CompilerParams(dimension_semantics=sem, vmem_limit_bytes=vmem_mib * MIB,
                                flags=flags)


def _inproj_kernel(x_ref, g_ref, w_ref, c_ref, sa_ref, sb_ref, qkv_ref, gate_ref):
    x = x_ref[...]
    ms = jnp.mean(x * x, axis=-1, keepdims=True)
    n = (x * lax.rsqrt(ms + EPS) * g_ref[...]).astype(BF16)
    chunk = 512
    reps = chunk // LANES
    cos = jnp.tile(c_ref[...], (1, reps))
    sin_up = jnp.tile(sa_ref[...], (1, reps))
    sin_dn = jnp.tile(sb_ref[...], (1, reps))
    half = ROPE_DIM // 2
    for j in range(QKV_W // chunk):
        y = jnp.dot(n, w_ref[:, j * chunk:(j + 1) * chunk], preferred_element_type=F32)
        if j < 2 * A_QK_W // chunk:
            y = (y * cos + pltpu.roll(y, half, axis=1) * sin_up
                 + pltpu.roll(y, chunk - half, axis=1) * sin_dn)
            if j < A_QK_W // chunk:
                y = y * (DIFF_HEAD_DIM ** -0.5)
        qkv_ref[:, j * chunk:(j + 1) * chunk] = y.astype(BF16)
    for j in range(GATE_W // chunk):
        c0 = QKV_W + j * chunk
        gate_ref[:, j * chunk:(j + 1) * chunk] = jnp.dot(
            n, w_ref[:, c0:c0 + chunk], preferred_element_type=F32)


def _inproj(x2, g, w_bf, cos_t, sin_up_t, sin_dn_t, seq):
    t = x2.shape[0]
    tm = TM_INPROJ
    nblk = seq // tm
    in_w = w_bf.shape[1]
    tab_spec = pl.BlockSpec((tm, LANES), lambda i: (i % nblk, 0))
    return pl.pallas_call(
        _inproj_kernel,
        grid=(t // tm,),
        in_specs=[
            pl.BlockSpec((tm, D_MODEL), lambda i: (i, 0)),
            pl.BlockSpec((1, D_MODEL), lambda i: (0, 0)),
            pl.BlockSpec((D_MODEL, in_w), lambda i: (0, 0)),
            tab_spec, tab_spec, tab_spec,
        ],
        out_specs=[
            pl.BlockSpec((tm, QKV_W), lambda i: (i, 0)),
            pl.BlockSpec((tm, GATE_W), lambda i: (i, 0)),
        ],
        out_shape=[
            jax.ShapeDtypeStruct((t, QKV_W), BF16),
            jax.ShapeDtypeStruct((t, GATE_W), F32),
        ],
        compiler_params=_cparams(("arbitrary",), 56),
        name="inproj",
    )(x2, g, w_bf, cos_t, sin_up_t, sin_dn_t)


def _diff_attn_kernel(lam_init, q_ref, k_ref, v_ref, lq1_ref, lk1_ref, lq2_ref, lk2_ref,
                      g_ref, o_ref):
    q = q_ref[...]
    k = k_ref[...]
    lane = lax.broadcasted_iota(jnp.int32, q.shape, 1)
    zero = jnp.zeros_like(q)
    q1 = jnp.where(lane < DIFF_HEAD_DIM, q, zero)
    q2 = jnp.where(lane >= DIFF_HEAD_DIM, q, zero)
    nt = (((1,), (1,)), ((), ()))
    s1 = lax.dot_general(q1, k, nt, preferred_element_type=F32)
    s2 = lax.dot_general(q2, k, nt, preferred_element_type=F32)
    e1 = jnp.exp(s1 - jnp.max(s1, axis=-1, keepdims=True))
    e2 = jnp.exp(s2 - jnp.max(s2, axis=-1, keepdims=True))
    l1 = jnp.sum(e1, axis=-1, keepdims=True)
    l2 = jnp.sum(e2, axis=-1, keepdims=True)
    lam = (jnp.exp(jnp.sum(lq1_ref[...] * lk1_ref[...], axis=-1, keepdims=True))
           - jnp.exp(jnp.sum(lq2_ref[...] * lk2_ref[...], axis=-1, keepdims=True))
           + lam_init)
    a = e1 * (1.0 / l1) - e2 * (lam / l2)
    o = jnp.dot(a.astype(BF16), v_ref[...], preferred_element_type=F32)
    ms = jnp.mean(o * o, axis=-1, keepdims=True)
    o = o * lax.rsqrt(ms + EPS) * g_ref[...] * (1.0 - lam_init)
    o_ref[...] = o.astype(o_ref.dtype)


def _diff_attn(qkv3, lq1, lk1, lq2, lk2, subln_g, lam_init):
    b, s, _ = qkv3.shape
    tq = TQ_DIFF
    hw = 2 * DIFF_HEAD_DIM
    k_blk0 = A_QK_W // hw
    v_blk0 = 2 * A_QK_W // hw
    vec = pl.BlockSpec((1, DIFF_HEAD_DIM), lambda bi, h, i: (0, 0))
    return pl.pallas_call(
        functools.partial(_diff_attn_kernel, lam_init),
        grid=(b, N_DIFF_HEADS, s // tq),
        in_specs=[
            pl.BlockSpec((None, tq, hw), lambda bi, h, i: (bi, i, h)),
            pl.BlockSpec((None, s, hw), lambda bi, h, i: (bi, 0, k_blk0 + h)),
            pl.BlockSpec((None, s, DIFF_V_DIM), lambda bi, h, i: (bi, 0, v_blk0 + h)),
            vec, vec, vec, vec,
            pl.BlockSpec((1, DIFF_V_DIM), lambda bi, h, i: (0, 0)),
        ],
        out_specs=pl.BlockSpec((None, tq, DIFF_V_DIM), lambda bi, h, i: (bi, i, h)),
        out_shape=jax.ShapeDtypeStruct((b, s, A_V_W), BF16),
        compiler_params=_cparams(("arbitrary", "arbitrary", "arbitrary"), 48),
        name="diff_attn",
    )(qkv3, qkv3, qkv3, lq1, lk1, lq2, lk2, subln_g)


def _na_window_start(r, rows, kr):
    return jnp.clip(r - kr // 2, 0, rows - kr)


def _na_kernel(rows, kr, q_ref, k_ref, v_ref, bias_ref, o_ref):
    r = pl.program_id(1)
    start = pl.multiple_of(_na_window_start(r, rows, kr) * GRID_W, GRID_W)
    nkeys = kr * GRID_W
    kwin = k_ref[pl.ds(start, nkeys), :]
    vwin = v_ref[pl.ds(start, nkeys), :]
    q = q_ref[...] * (NA_HEAD_DIM ** -0.5)
    nt = (((1,), (1,)), ((), ()))
    pair_w = 2 * NA_HEAD_DIM
    lane = lax.broadcasted_iota(jnp.int32, (GRID_W, pair_w), 1)
    for hp in range(N_NA_HEADS // 2):
        sl = slice(hp * pair_w, (hp + 1) * pair_w)
        qp, kp, vp = q[:, sl], kwin[:, sl], vwin[:, sl]
        outs = []
        for hh in range(2):
            keep = (lane < NA_HEAD_DIM) if hh == 0 else (lane >= NA_HEAD_DIM)
            qm = jnp.where(keep, qp, jnp.zeros_like(qp))
            s = lax.dot_general(qm, kp, nt, preferred_element_type=F32)
            s = s + bias_ref[2 * hp + hh]
            e = jnp.exp(s - jnp.max(s, axis=-1, keepdims=True))
            p = e * (1.0 / jnp.sum(e, axis=-1, keepdims=True))
            outs.append(jnp.dot(p.astype(BF16), vp, preferred_element_type=F32))
        o_ref[:, sl] = jnp.where(lane < NA_HEAD_DIM, outs[0], outs[1]).astype(o_ref.dtype)


def _na_attn(qkv3, bias_tab, rows, kr):
    b, s, _ = qkv3.shape
    q_blk = (2 * A_QK_W + A_V_W) // B_W
    nkeys = kr * GRID_W

    def bias_idx(bi, r):
        return (0, r - _na_window_start(r, rows, kr), 0, 0)

    return pl.pallas_call(
        functools.partial(_na_kernel, rows, kr),
        grid=(b, rows),
        in_specs=[
            pl.BlockSpec((None, GRID_W, B_W), lambda bi, r: (bi, r, q_blk)),
            pl.BlockSpec((None, s, B_W), lambda bi, r: (bi, 0, q_blk + 1)),
            pl.BlockSpec((None, s, B_W), lambda bi, r: (bi, 0, q_blk + 2)),
            pl.BlockSpec((N_NA_HEADS, None, GRID_W, nkeys), bias_idx),
        ],
        out_specs=pl.BlockSpec((None, GRID_W, B_W), lambda bi, r: (bi, r, 0)),
        out_shape=jax.ShapeDtypeStruct((b, s, B_W), BF16),
        compiler_params=_cparams(("arbitrary", "arbitrary"), 40),
        name="na_attn",
    )(qkv3, qkv3, qkv3, bias_tab)


def _na_bias_table(rpb, rows, kr):
    w = np.arange(GRID_W)[:, None]
    c = np.arange(GRID_W)[None, :]
    cs = np.clip(w - NA_KC // 2, 0, GRID_W - NA_KC)
    valid = (c >= cs) & (c < cs + NA_KC)
    ncol = 2 * NA_KC - 1
    onehot = ((c - w + (NA_KC - 1))[None] == np.arange(ncol)[:, None, None]) & valid[None]
    top = NA_KR_MAX - 1
    rsel = jnp.stack([rpb[:, top - d:top - d + kr, :] for d in range(kr)], axis=1).astype(F32)
    tab = jnp.einsum('hdik,kwc->hdwic', rsel, jnp.asarray(onehot, F32),
                     precision=lax.Precision.HIGHEST)
    tab = jnp.where(valid[None, None, :, None, :], tab, NEG_BIG)
    return tab.reshape(rpb.shape[0], kr, GRID_W, kr * GRID_W)


def _merge_kernel(ya_ref, yb_ref, gate_ref, x_ref, wpa_ref, wpb_ref, wout_ref, g_ref, wq_ref,
                  keys_ref, h_ref, xnt_ref, st_ref):
    ma = jnp.dot(ya_ref[...], wpa_ref[...], preferred_element_type=F32)
    mb = jnp.dot(yb_ref[...], wpb_ref[...], preferred_element_type=F32)
    ga = gate_ref[:, :D_MODEL]
    gb = gate_ref[:, D_MODEL:]
    merged = jax.nn.sigmoid(ga) * ma + jax.nn.sigmoid(gb) * mb
    h = x_ref[...] + jnp.dot(merged.astype(BF16), wout_ref[...], preferred_element_type=F32)
    h_ref[...] = h
    ms = jnp.mean(h * h, axis=-1, keepdims=True)
    xn_f32 = h * lax.rsqrt(ms + EPS) * g_ref[...]
    xn = xn_f32.astype(BF16)
    xnt_ref[...] = xn_f32.T.astype(BF16)
    q = jnp.dot(xn, wq_ref[...], preferred_element_type=F32).astype(BF16)
    nt = (((1,), (1,)), ((), ()))
    half = PEER_QUERY_DIM // 2
    for hp in range(2 * PEER_HEADS):
        st_ref[hp] = lax.dot_general(keys_ref[hp], q[:, hp * half:(hp + 1) * half], nt,
                                     preferred_element_type=F32)


def _merge(ya, yb, gates, x2, wpa, wpb, wout, g_ffn, wq, keys):
    t = x2.shape[0]
    tm = TM_MERGE
    qw = wq.shape[1]
    nlist = keys.shape[0]
    full = lambda shape: pl.BlockSpec(shape, lambda i: (0,) * len(shape))
    return pl.pallas_call(
        _merge_kernel,
        grid=(t // tm,),
        in_specs=[
            pl.BlockSpec((tm, A_V_W), lambda i: (i, 0)),
            pl.BlockSpec((tm, B_W), lambda i: (i, 0)),
            pl.BlockSpec((tm, GATE_W), lambda i: (i, 0)),
            pl.BlockSpec((tm, D_MODEL), lambda i: (i, 0)),
            full((A_V_W, D_MODEL)), full((B_W, D_MODEL)), full((D_MODEL, D_MODEL)),
            full((1, D_MODEL)), full((D_MODEL, qw)), full(keys.shape),
        ],
        out_specs=[
            pl.BlockSpec((tm, D_MODEL), lambda i: (i, 0)),
            pl.BlockSpec((D_MODEL, tm), lambda i: (0, i)),
            pl.BlockSpec((nlist, PEER_N_KEYS, tm), lambda i: (0, 0, i)),
        ],
        out_shape=[
            jax.ShapeDtypeStruct((t, D_MODEL), F32),
            jax.ShapeDtypeStruct((D_MODEL, t), BF16),
            jax.ShapeDtypeStruct((nlist, PEER_N_KEYS, t), F32),
        ],
        compiler_params=_cparams(("arbitrary",), 56),
        name="merge_query",
    )(ya, yb, gates, x2, wpa, wpb, wout, g_ffn, wq, keys)


def _staircase_layout():
    k = PEER_TOPK
    blocks = []
    for r1 in range(8):
        blocks.append((r1, 16 if r1 == 0 else 8, k // (r1 + 1)))
    pos, valid = [], []
    for r1, rows, width in blocks:
        for r2 in range(rows):
            pos.append(r1 * k + r2)
            valid.append(r2 < width)
    for r1 in range(8, k):
        pos.append(r1 * k)
        valid.append(True)
    return blocks, np.asarray(pos, np.int32), np.asarray(valid, bool)


def _extract_step(it, cur, rank, srt, key_iota, row_iota):
    m = jnp.max(cur, axis=0, keepdims=True)
    first = jnp.min(jnp.where(cur == m, key_iota, PEER_N_KEYS), axis=0, keepdims=True)
    sel = key_iota == first
    rank = jnp.where(sel, jnp.asarray(it, F32), rank)
    cur = jnp.where(sel, -jnp.inf, cur)
    srt = jnp.where(row_iota == it, m, srt)
    return cur, rank, srt


def _dup_bf16(x):
    bits = lax.bitcast_convert_type(x.astype(BF16).astype(F32), jnp.uint32)
    return bits | (bits >> 16)


def _select_kernel(pos_ref, valid_ref, s_ref, cnt_ref, ea_ref, rank2_ref, eb_ref):
    k = PEER_TOPK
    a = s_ref[0]
    b = s_ref[1]
    nk, nt = a.shape
    key_iota = lax.broadcasted_iota(jnp.int32, (nk, nt), 0)
    row_iota = lax.broadcasted_iota(jnp.int32, (k, nt), 0)
    unranked = jnp.full((nk, nt), float(nk), F32)
    srt0 = jnp.zeros((k, nt), F32)

    def body(it, carry):
        ca, ra, sa, cb, rb, sb = carry
        ca, ra, sa = _extract_step(it, ca, ra, sa, key_iota, row_iota)
        cb, rb, sb = _extract_step(it, cb, rb, sb, key_iota, row_iota)
        return ca, ra, sa, cb, rb, sb

    _, rank_a, as_, _, rank_b, bs_ = lax.fori_loop(
        0, k, body, (a, unranked, srt0, b, unranked, srt0))

    blocks, _, _ = _staircase_layout()
    parts = [as_[r1:r1 + 1] + bs_[0:rows] for r1, rows, _ in blocks]
    parts.append(as_[8:k] + bs_[0:1])
    cand0 = jnp.concatenate(parts, axis=0)
    pos = pos_ref[...]
    cand0 = jnp.where(valid_ref[...] > 0, cand0, -jnp.inf)
    far = k * k

    def pick(_, carry):
        cand, taken = carry
        m = jnp.max(cand, axis=0, keepdims=True)
        first = jnp.min(jnp.where(cand == m, pos, far), axis=0, keepdims=True)
        sel = pos == first
        return jnp.where(sel, -jnp.inf, cand), jnp.where(sel, 1.0, taken)

    _, taken = lax.fori_loop(0, k, pick, (cand0, jnp.zeros_like(cand0)))

    top = as_[0:1] + bs_[0:1]
    z = jnp.sum(taken * jnp.exp(cand0 - top), axis=0, keepdims=True)

    counts = []
    off = 0
    for _, rows, _ in blocks:
        counts.append(jnp.sum(taken[off:off + rows], axis=0, keepdims=True))
        off += rows
    for j in range(k - 8):
        counts.append(taken[off + j:off + j + 1])
    cnt = jnp.zeros((nk, nt), F32)
    for r1 in range(k):
        cnt = jnp.where(rank_a == float(r1), counts[r1], cnt)

    cnt_ref[...] = _dup_bf16(cnt)
    ea_ref[...] = _dup_bf16(jnp.exp(a - as_[0:1]))
    rank2_ref[...] = rank_b
    eb_ref[...] = jnp.exp(b - bs_[0:1]) * (1.0 / z)


def _select(scores_t):
    nlist, nk, t = scores_t.shape
    heads = nlist // 2
    _, pos, valid = _staircase_layout()
    ncand = pos.shape[0]
    pos_arr = jnp.asarray(np.broadcast_to(pos[:, None], (ncand, LANES)).copy())
    valid_arr = jnp.asarray(np.broadcast_to(valid[:, None].astype(np.int32), (ncand, LANES)).copy())
    tab = pl.BlockSpec((None, nk, LANES), lambda i, h: (h, 0, i))
    const = pl.BlockSpec((ncand, LANES), lambda i, h: (0, 0))
    dup = jax.ShapeDtypeStruct((heads, nk, t), jnp.uint32)
    half = jax.ShapeDtypeStruct((heads, nk, t), F32)
    return pl.pallas_call(
        _select_kernel,
        grid=(t // LANES, heads),
        in_specs=[const, const, pl.BlockSpec((2, nk, LANES), lambda i, h: (h, 0, i))],
        out_specs=[tab, tab, tab, tab],
        out_shape=[dup, dup, half, half],
        compiler_params=_cparams(("arbitrary", "arbitrary"), 32),
        name="peer_select",
    )(pos_arr, valid_arr, scores_t)


_ERFC_P = 0.3275911
_ERFC_HALF_A = tuple(0.5 * a for a in (0.254829592, -0.284496736, 1.421413741, -1.453152027,
                                       1.061405429))


def _gelu(x):
    z = x * math.sqrt(0.5)
    t = 1.0 / (1.0 + _ERFC_P * jnp.abs(z))
    poly = _ERFC_HALF_A[4]
    for a in _ERFC_HALF_A[3::-1]:
        poly = poly * t + a
    xh = x * (poly * t * jnp.exp(-(z * z)))
    return jnp.where(x >= 0, x - xh, xh)


def _peer_gate_tiles(j_rows, lane_chunks, cnt_ref, ea_ref, r2_ref, ebh_ref, at_ref, p_ref):
    for j in j_rows:
        rows = slice(j * PEER_N_KEYS, (j + 1) * PEER_N_KEYS)
        for tc in lane_chunks:
            cols = slice(tc * LANES, (tc + 1) * LANES)
            w = None
            for h in range(PEER_HEADS):
                bcast = lambda ref: pltpu.bitcast(
                    jnp.broadcast_to(ref[h, j:j + 1, cols], (PEER_N_KEYS // 2, LANES)), BF16)
                term = jnp.where(r2_ref[h, tc] < bcast(cnt_ref), ebh_ref[h, tc], 0.0) * bcast(ea_ref)
                w = term if w is None else w + term
            p_ref[rows, cols] = w * _gelu(at_ref[rows, cols]).astype(BF16)


def _peer_mix_kernel(final_norm, xnt_ref, u_ref, vt_ref, cnt_ref, ea_ref, rank2_ref, eb_ref,
                     h_ref, g_ref, o_ref, at_ref, p_ref, acc_ref, r2_ref, ebh_ref):
    e = pl.program_id(1)
    eb, tt = at_ref.shape

    @pl.when(e == 0)
    def _():
        acc_ref[...] = jnp.zeros_like(acc_ref)
        for h in range(PEER_HEADS):
            for tc in range(tt // LANES):
                cols = slice(tc * LANES, (tc + 1) * LANES)
                r2_ref[h, tc] = rank2_ref[h, :, cols].astype(BF16)
                ebh_ref[h, tc] = eb_ref[h, :, cols].astype(BF16)

    rows_per_sub = SB_PEER // PEER_N_KEYS
    for sb in range(eb // SB_PEER):
        sub = slice(sb * SB_PEER, (sb + 1) * SB_PEER)
        at_ref[sub, :] = jnp.dot(u_ref[sub, :], xnt_ref[...], preferred_element_type=F32)
        _peer_gate_tiles(range(sb * rows_per_sub, (sb + 1) * rows_per_sub), range(tt // LANES),
                         cnt_ref, ea_ref, r2_ref, ebh_ref, at_ref, p_ref)
    acc_ref[...] += jnp.dot(vt_ref[...], p_ref[...], preferred_element_type=F32)

    @pl.when(e == pl.num_programs(1) - 1)
    def _():
        y = h_ref[...] + acc_ref[...].T
        if final_norm:
            ms = jnp.mean(y * y, axis=-1, keepdims=True)
            y = y * lax.rsqrt(ms + EPS) * g_ref[...]
        o_ref[...] = y


def _peer_mix(xnt, u_bf, vt_bf, cnt, ea, rank2, eb, h, g_final, final_norm):
    t = xnt.shape[1]
    tt, ebk = TT_PEER, EB_PEER
    n_exp = u_bf.shape[0]
    heads, nk, _ = cnt.shape
    tab = pl.BlockSpec((heads, nk, tt), lambda ti, e: (0, 0, ti))
    row_tab = pl.BlockSpec((heads, ebk // nk, tt), lambda ti, e: (0, e, ti))
    return pl.pallas_call(
        functools.partial(_peer_mix_kernel, final_norm),
        grid=(t // tt, n_exp // ebk),
        in_specs=[
            pl.BlockSpec((D_MODEL, tt), lambda ti, e: (0, ti)),
            pl.BlockSpec((ebk, D_MODEL), lambda ti, e: (e, 0)),
            pl.BlockSpec((D_MODEL, ebk), lambda ti, e: (0, e)),
            row_tab, row_tab, tab, tab,
            pl.BlockSpec((tt, D_MODEL), lambda ti, e: (ti, 0)),
            pl.BlockSpec((1, D_MODEL), lambda ti, e: (0, 0)),
        ],
        out_specs=pl.BlockSpec((tt, D_MODEL), lambda ti, e: (ti, 0)),
        out_shape=jax.ShapeDtypeStruct((t, D_MODEL), F32),
        scratch_shapes=[
            pltpu.VMEM((ebk, tt), F32),
            pltpu.VMEM((ebk, tt), BF16),
            pltpu.VMEM((D_MODEL, tt), F32),
            pltpu.VMEM((heads, tt // LANES, nk, LANES), BF16),
            pltpu.VMEM((heads, tt // LANES, nk, LANES), BF16),
        ],
        compiler_params=_cparams(("arbitrary", "arbitrary"), 56),
        name="peer_mix",
    )(xnt, u_bf, vt_bf, cnt, ea, rank2, eb, h, g_final)


def _rope_tables(seq):
    half = ROPE_DIM // 2
    inv_freq = ROPE_THETA ** (-jnp.arange(0, ROPE_DIM, 2, dtype=F32) / ROPE_DIM)
    ang = jnp.arange(seq, dtype=F32)[:, None] * inv_freq[None, :]
    cos, sin = jnp.cos(ang), jnp.sin(ang)
    pad = jnp.zeros((seq, DIFF_HEAD_DIM - ROPE_DIM), F32)
    c64 = jnp.concatenate([cos, cos, pad + 1.0], axis=1)
    up64 = jnp.concatenate([jnp.zeros_like(sin), sin, pad], axis=1)
    dn64 = jnp.concatenate([-sin, jnp.zeros_like(sin), pad], axis=1)
    rep = LANES // DIFF_HEAD_DIM
    return jnp.tile(c64, (1, rep)), jnp.tile(up64, (1, rep)), jnp.tile(dn64, (1, rep))


def kernel(x, w_in, w_proj_a, w_proj_b, w_out, norm_mix, norm_ffn, norm_final, lambda_q1,
           lambda_k1, lambda_q2, lambda_k2, diff_subln, na_rpb, peer_w_query, peer_sub_keys,
           peer_u, peer_v):
    b, s, d = x.shape
    t = b * s
    depth = w_in.shape[0]
    rows = s // GRID_W
    kr = min(NA_KR_MAX, rows)
    cos_t, sin_up_t, sin_dn_t = _rope_tables(s)
    h = x.reshape(t, d)
    row = lambda v: v.reshape(1, -1).astype(F32)
    for l in range(depth):
        lam_init = 0.8 - 0.6 * math.exp(-0.3 * l)
        qkv, gates = _inproj(h, row(norm_mix[l]), w_in[l].astype(BF16), cos_t, sin_up_t,
                             sin_dn_t, s)
        qkv3 = qkv.reshape(b, s, QKV_W)
        ya = _diff_attn(qkv3, row(lambda_q1[l]), row(lambda_k1[l]), row(lambda_q2[l]),
                        row(lambda_k2[l]), row(diff_subln[l]), lam_init)
        yb = _na_attn(qkv3, _na_bias_table(na_rpb[l], rows, kr), rows, kr)
        keys = peer_sub_keys[l].reshape(2 * PEER_HEADS, PEER_N_KEYS, PEER_QUERY_DIM // 2)
        h_mid, xnt, scores_t = _merge(
            ya.reshape(t, A_V_W), yb.reshape(t, B_W), gates, h,
            w_proj_a[l].astype(BF16), w_proj_b[l].astype(BF16), w_out[l].astype(BF16),
            row(norm_ffn[l]), peer_w_query[l].astype(BF16), keys.astype(BF16))
        cnt, ea, rank2, eb = _select(scores_t)
        h = _peer_mix(xnt, peer_u[l].astype(BF16), peer_v[l].T.astype(BF16), cnt, ea, rank2, eb,
                      h_mid, row(norm_final), final_norm=(l == depth - 1))
    return h.reshape(b, s, d)
```

```python
import functools
import math

import jax
import jax.numpy as jnp
import numpy as np
from jax import lax
from jax.experimental import pallas as pl
from jax.experimental.pallas import tpu as pltpu

F32 = jnp.float32
BF16 = jnp.bfloat16

D_MODEL = 1024
GRID_W = 64
N_DIFF_HEADS = 4
DIFF_HEAD_DIM = 64
DIFF_V_DIM = 2 * DIFF_HEAD_DIM
ROPE_THETA = 500000.0
ROPE_DIM = DIFF_HEAD_DIM // 4
N_NA_HEADS = 8
NA_HEAD_DIM = 64
NA_KR_MAX = 8
NA_KC = 16
A_QK_W = N_DIFF_HEADS * 2 * DIFF_HEAD_DIM
A_V_W = N_DIFF_HEADS * DIFF_V_DIM
B_W = N_NA_HEADS * NA_HEAD_DIM
QKV_W = 2 * A_QK_W + A_V_W + 3 * B_W
GATE_W = 2 * D_MODEL
PEER_HEADS = 8
PEER_N_KEYS = 128
PEER_N_EXPERTS = PEER_N_KEYS * PEER_N_KEYS
PEER_QUERY_DIM = 256
PEER_TOPK = 16
EPS = 1e-6

LANES = 128
NEG_BIG = -1e30
MIB = 1024 * 1024

TM_INPROJ = 512
TQ_DIFF = 256
TM_MERGE = 256
TT_PEER = 512
EB_PEER = 1024
SB_PEER = 512
BF16_SUBLANE_ROWS = 16
TAB_EB_ROW = PEER_N_KEYS + BF16_SUBLANE_ROWS
TAB_ROWS = TAB_EB_ROW + PEER_N_KEYS


def _cparams(sem, vmem_mib, flags=None):
    return pltpu.CompilerParams(dimension_semantics=sem, vmem_limit_bytes=vmem_mib * MIB,
                                flags=flags)


def _inproj_kernel(x_ref, g_ref, w_ref, c_ref, sa_ref, sb_ref, qkv_ref, gate_ref):
    x = x_ref[...]
    ms = jnp.mean(x * x, axis=-1, keepdims=True)
    n = (x * lax.rsqrt(ms + EPS) * g_ref[...]).astype(BF16)
    chunk = 512
    reps = chunk // LANES
    cos = jnp.tile(c_ref[...], (1, reps))
    sin_up = jnp.tile(sa_ref[...], (1, reps))
    sin_dn = jnp.tile(sb_ref[...], (1, reps))
    half = ROPE_DIM // 2
    for j in range(QKV_W // chunk):
        y = jnp.dot(n, w_ref[:, j * chunk:(j + 1) * chunk], preferred_element_type=F32)
        if j < 2 * A_QK_W // chunk:
            y = (y * cos + pltpu.roll(y, half, axis=1) * sin_up
                 + pltpu.roll(y, chunk - half, axis=1) * sin_dn)
            if j < A_QK_W // chunk:
                y = y * (DIFF_HEAD_DIM ** -0.5)
        qkv_ref[:, j * chunk:(j + 1) * chunk] = y.astype(BF16)
    for j in range(GATE_W // chunk):
        c0 = QKV_W + j * chunk
        gate_ref[:, j * chunk:(j + 1) * chunk] = jnp.dot(
            n, w_ref[:, c0:c0 + chunk], preferred_element_type=F32)


def _inproj(x2, g, w_bf, cos_t, sin_up_t, sin_dn_t, seq):
    t = x2.shape[0]
    tm = TM_INPROJ
    nblk = seq // tm
    in_w = w_bf.shape[1]
    tab_spec = pl.BlockSpec((tm, LANES), lambda i: (i % nblk, 0))
    return pl.pallas_call(
        _inproj_kernel,
        grid=(t // tm,),
        in_specs=[
            pl.BlockSpec((tm, D_MODEL), lambda i: (i, 0)),
            pl.BlockSpec((1, D_MODEL), lambda i: (0, 0)),
            pl.BlockSpec((D_MODEL, in_w), lambda i: (0, 0)),
            tab_spec, tab_spec, tab_spec,
        ],
        out_specs=[
            pl.BlockSpec((tm, QKV_W), lambda i: (i, 0)),
            pl.BlockSpec((tm, GATE_W), lambda i: (i, 0)),
        ],
        out_shape=[
            jax.ShapeDtypeStruct((t, QKV_W), BF16),
            jax.ShapeDtypeStruct((t, GATE_W), F32),
        ],
        compiler_params=_cparams(("arbitrary",), 56),
        name="inproj",
    )(x2, g, w_bf, cos_t, sin_up_t, sin_dn_t)


def _diff_attn_kernel(lam_init, q_ref, k_ref, v_ref, lq1_ref, lk1_ref, lq2_ref, lk2_ref,
                      g_ref, o_ref):
    q = q_ref[...]
    k = k_ref[...]
    lane = lax.broadcasted_iota(jnp.int32, q.shape, 1)
    zero = jnp.zeros_like(q)
    q1 = jnp.where(lane < DIFF_HEAD_DIM, q, zero)
    q2 = jnp.where(lane >= DIFF_HEAD_DIM, q, zero)
    nt = (((1,), (1,)), ((), ()))
    s1 = lax.dot_general(q1, k, nt, preferred_element_type=F32)
    s2 = lax.dot_general(q2, k, nt, preferred_element_type=F32)
    e1 = jnp.exp(s1 - jnp.max(s1, axis=-1, keepdims=True))
    e2 = jnp.exp(s2 - jnp.max(s2, axis=-1, keepdims=True))
    l1 = jnp.sum(e1, axis=-1, keepdims=True)
    l2 = jnp.sum(e2, axis=-1, keepdims=True)
    lam = (jnp.exp(jnp.sum(lq1_ref[...] * lk1_ref[...], axis=-1, keepdims=True))
           - jnp.exp(jnp.sum(lq2_ref[...] * lk2_ref[...], axis=-1, keepdims=True))
           + lam_init)
    a = e1 * (1.0 / l1) - e2 * (lam / l2)
    o = jnp.dot(a.astype(BF16), v_ref[...], preferred_element_type=F32)
    ms = jnp.mean(o * o, axis=-1, keepdims=True)
    o = o * lax.rsqrt(ms + EPS) * g_ref[...] * (1.0 - lam_init)
    o_ref[...] = o.astype(o_ref.dtype)


def _diff_attn(qkv3, lq1, lk1, lq2, lk2, subln_g, lam_init):
    b, s, _ = qkv3.shape
    tq = TQ_DIFF
    hw = 2 * DIFF_HEAD_DIM
    k_blk0 = A_QK_W // hw
    v_blk0 = 2 * A_QK_W // hw
    vec = pl.BlockSpec((1, DIFF_HEAD_DIM), lambda bi, h, i: (0, 0))
    return pl.pallas_call(
        functools.partial(_diff_attn_kernel, lam_init),
        grid=(b, N_DIFF_HEADS, s // tq),
        in_specs=[
            pl.BlockSpec((None, tq, hw), lambda bi, h, i: (bi, i, h)),
            pl.BlockSpec((None, s, hw), lambda bi, h, i: (bi, 0, k_blk0 + h)),
            pl.BlockSpec((None, s, DIFF_V_DIM), lambda bi, h, i: (bi, 0, v_blk0 + h)),
            vec, vec, vec, vec,
            pl.BlockSpec((1, DIFF_V_DIM), lambda bi, h, i: (0, 0)),
        ],
        out_specs=pl.BlockSpec((None, tq, DIFF_V_DIM), lambda bi, h, i: (bi, i, h)),
        out_shape=jax.ShapeDtypeStruct((b, s, A_V_W), BF16),
        compiler_params=_cparams(("arbitrary", "arbitrary", "arbitrary"), 48),
        name="diff_attn",
    )(qkv3, qkv3, qkv3, lq1, lk1, lq2, lk2, subln_g)


def _na_window_start(r, rows, kr):
    return jnp.clip(r - kr // 2, 0, rows - kr)


def _na_kernel(rows, kr, q_ref, k_ref, v_ref, bias_ref, o_ref):
    r = pl.program_id(1)
    start = pl.multiple_of(_na_window_start(r, rows, kr) * GRID_W, GRID_W)
    nkeys = kr * GRID_W
    kwin = k_ref[pl.ds(start, nkeys), :]
    vwin = v_ref[pl.ds(start, nkeys), :]
    q = q_ref[...] * (NA_HEAD_DIM ** -0.5)
    nt = (((1,), (1,)), ((), ()))
    pair_w = 2 * NA_HEAD_DIM
    lane = lax.broadcasted_iota(jnp.int32, (GRID_W, pair_w), 1)
    for hp in range(N_NA_HEADS // 2):
        sl = slice(hp * pair_w, (hp + 1) * pair_w)
        qp, kp, vp = q[:, sl], kwin[:, sl], vwin[:, sl]
        outs = []
        for hh in range(2):
            keep = (lane < NA_HEAD_DIM) if hh == 0 else (lane >= NA_HEAD_DIM)
            qm = jnp.where(keep, qp, jnp.zeros_like(qp))
            s = lax.dot_general(qm, kp, nt, preferred_element_type=F32)
            s = s + bias_ref[2 * hp + hh]
            e = jnp.exp(s - jnp.max(s, axis=-1, keepdims=True))
            p = e * (1.0 / jnp.sum(e, axis=-1, keepdims=True))
            outs.append(jnp.dot(p.astype(BF16), vp, preferred_element_type=F32))
        o_ref[:, sl] = jnp.where(lane < NA_HEAD_DIM, outs[0], outs[1]).astype(o_ref.dtype)


def _na_attn(qkv3, bias_tab, rows, kr):
    b, s, _ = qkv3.shape
    q_blk = (2 * A_QK_W + A_V_W) // B_W
    nkeys = kr * GRID_W

    def bias_idx(bi, r):
        return (0, r - _na_window_start(r, rows, kr), 0, 0)

    return pl.pallas_call(
        functools.partial(_na_kernel, rows, kr),
        grid=(b, rows),
        in_specs=[
            pl.BlockSpec((None, GRID_W, B_W), lambda bi, r: (bi, r, q_blk)),
            pl.BlockSpec((None, s, B_W), lambda bi, r: (bi, 0, q_blk + 1)),
            pl.BlockSpec((None, s, B_W), lambda bi, r: (bi, 0, q_blk + 2)),
            pl.BlockSpec((N_NA_HEADS, None, GRID_W, nkeys), bias_idx),
        ],
        out_specs=pl.BlockSpec((None, GRID_W, B_W), lambda bi, r: (bi, r, 0)),
        out_shape=jax.ShapeDtypeStruct((b, s, B_W), BF16),
        compiler_params=_cparams(("arbitrary", "arbitrary"), 40),
        name="na_attn",
    )(qkv3, qkv3, qkv3, bias_tab)


def _na_bias_table(rpb, rows, kr):
    w = np.arange(GRID_W)[:, None]
    c = np.arange(GRID_W)[None, :]
    cs = np.clip(w - NA_KC // 2, 0, GRID_W - NA_KC)
    valid = (c >= cs) & (c < cs + NA_KC)
    ncol = 2 * NA_KC - 1
    onehot = ((c - w + (NA_KC - 1))[None] == np.arange(ncol)[:, None, None]) & valid[None]
    top = NA_KR_MAX - 1
    rsel = jnp.stack([rpb[:, top - d:top - d + kr, :] for d in range(kr)], axis=1).astype(F32)
    tab = jnp.einsum('hdik,kwc->hdwic', rsel, jnp.asarray(onehot, F32),
                     precision=lax.Precision.HIGHEST)
    tab = jnp.where(valid[None, None, :, None, :], tab, NEG_BIG)
    return tab.reshape(rpb.shape[0], kr, GRID_W, kr * GRID_W)


def _merge_kernel(ya_ref, yb_ref, gate_ref, x_ref, wpa_ref, wpb_ref, wout_ref, g_ref, wq_ref,
                  keys_ref, h_ref, xnt_ref, st_ref):
    ma = jnp.dot(ya_ref[...], wpa_ref[...], preferred_element_type=F32)
    mb = jnp.dot(yb_ref[...], wpb_ref[...], preferred_element_type=F32)
    ga = gate_ref[:, :D_MODEL]
    gb = gate_ref[:, D_MODEL:]
    merged = jax.nn.sigmoid(ga) * ma + jax.nn.sigmoid(gb) * mb
    h = x_ref[...] + jnp.dot(merged.astype(BF16), wout_ref[...], preferred_element_type=F32)
    h_ref[...] = h
    ms = jnp.mean(h * h, axis=-1, keepdims=True)
    xn_f32 = h * lax.rsqrt(ms + EPS) * g_ref[...]
    xn = xn_f32.astype(BF16)
    xnt_ref[...] = xn_f32.T.astype(BF16)
    q = jnp.dot(xn, wq_ref[...], preferred_element_type=F32).astype(BF16)
    nt = (((1,), (1,)), ((), ()))
    half = PEER_QUERY_DIM // 2
    for hp in range(2 * PEER_HEADS):
        st_ref[hp] = lax.dot_general(keys_ref[hp], q[:, hp * half:(hp + 1) * half], nt,
                                     preferred_element_type=F32)


def _merge(ya, yb, gates, x2, wpa, wpb, wout, g_ffn, wq, keys):
    t = x2.shape[0]
    tm = TM_MERGE
    qw = wq.shape[1]
    nlist = keys.shape[0]
    full = lambda shape: pl.BlockSpec(shape, lambda i: (0,) * len(shape))
    return pl.pallas_call(
        _merge_kernel,
        grid=(t // tm,),
        in_specs=[
            pl.BlockSpec((tm, A_V_W), lambda i: (i, 0)),
            pl.BlockSpec((tm, B_W), lambda i: (i, 0)),
            pl.BlockSpec((tm, GATE_W), lambda i: (i, 0)),
            pl.BlockSpec((tm, D_MODEL), lambda i: (i, 0)),
            full((A_V_W, D_MODEL)), full((B_W, D_MODEL)), full((D_MODEL, D_MODEL)),
            full((1, D_MODEL)), full((D_MODEL, qw)), full(keys.shape),
        ],
        out_specs=[
            pl.BlockSpec((tm, D_MODEL), lambda i: (i, 0)),
            pl.BlockSpec((D_MODEL, tm), lambda i: (0, i)),
            pl.BlockSpec((nlist, PEER_N_KEYS, tm), lambda i: (0, 0, i)),
        ],
        out_shape=[
            jax.ShapeDtypeStruct((t, D_MODEL), F32),
            jax.ShapeDtypeStruct((D_MODEL, t), BF16),
            jax.ShapeDtypeStruct((nlist, PEER_N_KEYS, t), F32),
        ],
        compiler_params=_cparams(("arbitrary",), 56),
        name="merge_query",
    )(ya, yb, gates, x2, wpa, wpb, wout, g_ffn, wq, keys)


def _staircase_layout():
    k = PEER_TOPK
    blocks = []
    for r1 in range(8):
        blocks.append((r1, 16 if r1 == 0 else 8, k // (r1 + 1)))
    pos, valid = [], []
    for r1, rows, width in blocks:
        for r2 in range(rows):
            pos.append(r1 * k + r2)
            valid.append(r2 < width)
    for r1 in range(8, k):
        pos.append(r1 * k)
        valid.append(True)
    return blocks, np.asarray(pos, np.int32), np.asarray(valid, bool)


def _extract_step(it, cur, rank, srt, key_iota, row_iota):
    m = jnp.max(cur, axis=0, keepdims=True)
    first = jnp.min(jnp.where(cur == m, key_iota, PEER_N_KEYS), axis=0, keepdims=True)
    sel = key_iota == first
    rank = jnp.where(sel, jnp.asarray(it, F32), rank)
    cur = jnp.where(sel, -jnp.inf, cur)
    srt = jnp.where(row_iota == it, m, srt)
    return cur, rank, srt


def _select_kernel(pos_ref, valid_ref, s_ref, cnt_ref, ea_ref, rank2_ref, eb_ref):
    k = PEER_TOPK
    a = s_ref[0]
    b = s_ref[1]
    nk, nt = a.shape
    key_iota = lax.broadcasted_iota(jnp.int32, (nk, nt), 0)
    row_iota = lax.broadcasted_iota(jnp.int32, (k, nt), 0)
    unranked = jnp.full((nk, nt), float(nk), F32)
    srt0 = jnp.zeros((k, nt), F32)

    def body(it, carry):
        ca, ra, sa, cb, rb, sb = carry
        ca, ra, sa = _extract_step(it, ca, ra, sa, key_iota, row_iota)
        cb, rb, sb = _extract_step(it, cb, rb, sb, key_iota, row_iota)
        return ca, ra, sa, cb, rb, sb

    _, rank_a, as_, _, rank_b, bs_ = lax.fori_loop(
        0, k, body, (a, unranked, srt0, b, unranked, srt0))

    blocks, _, _ = _staircase_layout()
    parts = [as_[r1:r1 + 1] + bs_[0:rows] for r1, rows, _ in blocks]
    parts.append(as_[8:k] + bs_[0:1])
    cand0 = jnp.concatenate(parts, axis=0)
    pos = pos_ref[...]
    cand0 = jnp.where(valid_ref[...] > 0, cand0, -jnp.inf)
    far = k * k

    def pick(_, carry):
        cand, taken = carry
        m = jnp.max(cand, axis=0, keepdims=True)
        first = jnp.min(jnp.where(cand == m, pos, far), axis=0, keepdims=True)
        sel = pos == first
        return jnp.where(sel, -jnp.inf, cand), jnp.where(sel, 1.0, taken)

    _, taken = lax.fori_loop(0, k, pick, (cand0, jnp.zeros_like(cand0)))

    top = as_[0:1] + bs_[0:1]
    z = jnp.sum(taken * jnp.exp(cand0 - top), axis=0, keepdims=True)

    counts = []
    off = 0
    for _, rows, _ in blocks:
        counts.append(jnp.sum(taken[off:off + rows], axis=0, keepdims=True))
        off += rows
    for j in range(k - 8):
        counts.append(taken[off + j:off + j + 1])
    cnt = jnp.zeros((nk, nt), F32)
    for r1 in range(k):
        cnt = jnp.where(rank_a == float(r1), counts[r1], cnt)

    cnt_ref[...] = cnt
    ea_ref[...] = jnp.exp(a - as_[0:1])
    rank2_ref[...] = rank_b
    eb_ref[...] = jnp.exp(b - bs_[0:1]) * (1.0 / z)


def _select(scores_t):
    nlist, nk, t = scores_t.shape
    heads = nlist // 2
    _, pos, valid = _staircase_layout()
    ncand = pos.shape[0]
    pos_arr = jnp.asarray(np.broadcast_to(pos[:, None], (ncand, LANES)).copy())
    valid_arr = jnp.asarray(np.broadcast_to(valid[:, None].astype(np.int32), (ncand, LANES)).copy())
    tab = pl.BlockSpec((None, nk, LANES), lambda i, h: (h, 0, i))
    const = pl.BlockSpec((ncand, LANES), lambda i, h: (0, 0))
    shape = jax.ShapeDtypeStruct((heads, nk, t), F32)
    return pl.pallas_call(
        _select_kernel,
        grid=(t // LANES, heads),
        in_specs=[const, const, pl.BlockSpec((2, nk, LANES), lambda i, h: (h, 0, i))],
        out_specs=[tab, tab, tab, tab],
        out_shape=[shape, shape, shape, shape],
        compiler_params=_cparams(("arbitrary", "arbitrary"), 32),
        name="peer_select",
    )(pos_arr, valid_arr, scores_t)


_ERFC_P = 0.3275911
_ERFC_HALF_A = tuple(0.5 * a for a in (0.254829592, -0.284496736, 1.421413741, -1.453152027,
                                       1.061405429))


def _gelu(x):
    z = x * math.sqrt(0.5)
    t = 1.0 / (1.0 + _ERFC_P * jnp.abs(z))
    poly = _ERFC_HALF_A[4]
    for a in _ERFC_HALF_A[3::-1]:
        poly = poly * t + a
    xh = x * (poly * t * jnp.exp(-(z * z)))
    return jnp.where(x >= 0, x - xh, xh)


def _peer_gate_chunk(tc, j_rows, cnt_ref, ea_ref, tab_ref, at_ref, p_ref):
    cols = pl.ds(pl.multiple_of(tc * LANES, LANES), LANES)
    n_k = PEER_N_KEYS // BF16_SUBLANE_ROWS
    acc = [[None] * n_k for _ in j_rows]
    for h in range(PEER_HEADS):
        bcast = lambda ref, j: jnp.broadcast_to(
            ref[h, j:j + 1, cols], (BF16_SUBLANE_ROWS, LANES)).astype(BF16)
        cnt_b = [bcast(cnt_ref, j) for j in j_rows]
        ea_b = [bcast(ea_ref, j) for j in j_rows]
        for k in range(n_k):
            r0 = k * BF16_SUBLANE_ROWS
            r2 = tab_ref[h, tc, r0:r0 + BF16_SUBLANE_ROWS, :]
            ebh = tab_ref[h, tc, TAB_EB_ROW + r0:TAB_EB_ROW + r0 + BF16_SUBLANE_ROWS, :]
            for jj in range(len(j_rows)):
                term = jnp.where(r2 < cnt_b[jj], ebh, 0.0) * ea_b[jj]
                acc[jj][k] = term if acc[jj][k] is None else acc[jj][k] + term
    for jj, j in enumerate(j_rows):
        for k in range(n_k):
            rows = slice(j * PEER_N_KEYS + k * BF16_SUBLANE_ROWS,
                         j * PEER_N_KEYS + (k + 1) * BF16_SUBLANE_ROWS)
            act = at_ref[rows, cols]
            gelu = 0.5 * act * (1.0 + lax.erf(act * math.sqrt(0.5)))
            p_ref[rows, cols] = acc[jj][k] * gelu.astype(BF16)


def _peer_mix_kernel(final_norm, xnt_ref, u_ref, vt_ref, cnt_ref, ea_ref, rank2_ref, eb_ref,
                     h_ref, g_ref, o_ref, at_ref, p_ref, acc_ref, tab_ref):
    e = pl.program_id(1)
    eb, tt = p_ref.shape

    @pl.when(e == 0)
    def _():
        acc_ref[...] = jnp.zeros_like(acc_ref)
        for h in range(PEER_HEADS):
            for tc in range(tt // LANES):
                cols = slice(tc * LANES, (tc + 1) * LANES)
                tab_ref[h, tc, 0:PEER_N_KEYS, :] = rank2_ref[h, :, cols].astype(BF16)
                tab_ref[h, tc, TAB_EB_ROW:TAB_EB_ROW + PEER_N_KEYS, :] = eb_ref[h, :, cols].astype(BF16)

    rows_per_sub = SB_PEER // PEER_N_KEYS
    for sb in range(eb // SB_PEER):
        sub = slice(sb * SB_PEER, (sb + 1) * SB_PEER)
        at_ref[sub, 0:tt] = jnp.dot(u_ref[sub, :], xnt_ref[...], preferred_element_type=F32)
        j_rows = tuple(range(sb * rows_per_sub, (sb + 1) * rows_per_sub))

        def lane_chunk(tc, carry):
            _peer_gate_chunk(tc, j_rows, cnt_ref, ea_ref, tab_ref, at_ref, p_ref)
            return carry

        lax.fori_loop(0, tt // LANES, lane_chunk, 0)
    acc_ref[...] += jnp.dot(vt_ref[...], p_ref[...], preferred_element_type=F32)

    @pl.when(e == pl.num_programs(1) - 1)
    def _():
        y = h_ref[...] + acc_ref[...].T
        if final_norm:
            ms = jnp.mean(y * y, axis=-1, keepdims=True)
            y = y * lax.rsqrt(ms + EPS) * g_ref[...]
        o_ref[...] = y


def _peer_mix(xnt, u_bf, vt_bf, cnt, ea, rank2, eb, h, g_final, final_norm):
    t = xnt.shape[1]
    tt, ebk = TT_PEER, EB_PEER
    n_exp = u_bf.shape[0]
    heads, nk, _ = cnt.shape
    tab = pl.BlockSpec((heads, nk, tt), lambda ti, e: (0, 0, ti))
    row_tab = pl.BlockSpec((heads, ebk // nk, tt), lambda ti, e: (0, e, ti))
    return pl.pallas_call(
        functools.partial(_peer_mix_kernel, final_norm),
        grid=(t // tt, n_exp // ebk),
        in_specs=[
            pl.BlockSpec((D_MODEL, tt), lambda ti, e: (0, ti)),
            pl.BlockSpec((ebk, D_MODEL), lambda ti, e: (e, 0)),
            pl.BlockSpec((None, D_MODEL, ebk), lambda ti, e: (e, 0, 0)),
            row_tab, row_tab, tab, tab,
            pl.BlockSpec((tt, D_MODEL), lambda ti, e: (ti, 0)),
            pl.BlockSpec((1, D_MODEL), lambda ti, e: (0, 0)),
        ],
        out_specs=pl.BlockSpec((tt, D_MODEL), lambda ti, e: (ti, 0)),
        out_shape=jax.ShapeDtypeStruct((t, D_MODEL), F32),
        scratch_shapes=[
            pltpu.VMEM((ebk, tt + LANES), F32),
            pltpu.VMEM((ebk, tt), BF16),
            pltpu.VMEM((D_MODEL, tt), F32),
            pltpu.VMEM((heads, tt // LANES, TAB_ROWS, LANES), BF16),
        ],
        compiler_params=_cparams(("arbitrary", "arbitrary"), 56),
        name="peer_mix",
    )(xnt, u_bf, vt_bf, cnt, ea, rank2, eb, h, g_final)


def _rope_tables(seq):
    half = ROPE_DIM // 2
    inv_freq = ROPE_THETA ** (-jnp.arange(0, ROPE_DIM, 2, dtype=F32) / ROPE_DIM)
    ang = jnp.arange(seq, dtype=F32)[:, None] * inv_freq[None, :]
    cos, sin = jnp.cos(ang), jnp.sin(ang)
    pad = jnp.zeros((seq, DIFF_HEAD_DIM - ROPE_DIM), F32)
    c64 = jnp.concatenate([cos, cos, pad + 1.0], axis=1)
    up64 = jnp.concatenate([jnp.zeros_like(sin), sin, pad], axis=1)
    dn64 = jnp.concatenate([-sin, jnp.zeros_like(sin), pad], axis=1)
    rep = LANES // DIFF_HEAD_DIM
    return jnp.tile(c64, (1, rep)), jnp.tile(up64, (1, rep)), jnp.tile(dn64, (1, rep))


def kernel(x, w_in, w_proj_a, w_proj_b, w_out, norm_mix, norm_ffn, norm_final, lambda_q1,
           lambda_k1, lambda_q2, lambda_k2, diff_subln, na_rpb, peer_w_query, peer_sub_keys,
           peer_u, peer_v):
    b, s, d = x.shape
    t = b * s
    depth = w_in.shape[0]
    rows = s // GRID_W
    kr = min(NA_KR_MAX, rows)
    cos_t, sin_up_t, sin_dn_t = _rope_tables(s)
    h = x.reshape(t, d)
    row = lambda v: v.reshape(1, -1).astype(F32)
    for l in range(depth):
        lam_init = 0.8 - 0.6 * math.exp(-0.3 * l)
        qkv, gates = _inproj(h, row(norm_mix[l]), w_in[l].astype(BF16), cos_t, sin_up_t,
                             sin_dn_t, s)
        qkv3 = qkv.reshape(b, s, QKV_W)
        ya = _diff_attn(qkv3, row(lambda_q1[l]), row(lambda_k1[l]), row(lambda_q2[l]),
                        row(lambda_k2[l]), row(diff_subln[l]), lam_init)
        yb = _na_attn(qkv3, _na_bias_table(na_rpb[l], rows, kr), rows, kr)
        keys = peer_sub_keys[l].reshape(2 * PEER_HEADS, PEER_N_KEYS, PEER_QUERY_DIM // 2)
        h_mid, xnt, scores_t = _merge(
            ya.reshape(t, A_V_W), yb.reshape(t, B_W), gates, h,
            w_proj_a[l].astype(BF16), w_proj_b[l].astype(BF16), w_out[l].astype(BF16),
            row(norm_ffn[l]), peer_w_query[l].astype(BF16), keys.astype(BF16))
        cnt, ea, rank2, eb = _select(scores_t)
        vt_blocks = peer_v[l].reshape(-1, EB_PEER, d).transpose(0, 2, 1).astype(BF16)
        h = _peer_mix(xnt, peer_u[l].astype(BF16), vt_blocks, cnt, ea, rank2, eb,
                      h_mid, row(norm_final), final_norm=(l == depth - 1))
    return h.reshape(b, s, d)
```

```python
import functools
import math

import jax
import jax.numpy as jnp
import numpy as np
from jax import lax
from jax.experimental import pallas as pl
from jax.experimental.pallas import tpu as pltpu

F32 = jnp.float32
BF16 = jnp.bfloat16

D_MODEL = 1024
GRID_W = 64
N_DIFF_HEADS = 4
DIFF_HEAD_DIM = 64
DIFF_V_DIM = 2 * DIFF_HEAD_DIM
ROPE_THETA = 500000.0
ROPE_DIM = DIFF_HEAD_DIM // 4
N_NA_HEADS = 8
NA_HEAD_DIM = 64
NA_KR_MAX = 8
NA_KC = 16
A_QK_W = N_DIFF_HEADS * 2 * DIFF_HEAD_DIM
A_V_W = N_DIFF_HEADS * DIFF_V_DIM
B_W = N_NA_HEADS * NA_HEAD_DIM
QKV_W = 2 * A_QK_W + A_V_W + 3 * B_W
GATE_W = 2 * D_MODEL
PEER_HEADS = 8
PEER_N_KEYS = 128
PEER_N_EXPERTS = PEER_N_KEYS * PEER_N_KEYS
PEER_QUERY_DIM = 256
PEER_TOPK = 16
EPS = 1e-6

LANES = 128
NEG_BIG = -1e30
MIB = 1024 * 1024

TM_INPROJ = 512
TQ_DIFF = 256
TM_MERGE = 256
TT_PEER = 512
EB_PEER = 1024
SB_PEER = 512
BF16_SUBLANE_ROWS = 16
TAB_EB_ROW = PEER_N_KEYS + BF16_SUBLANE_ROWS
TAB_ROWS = TAB_EB_ROW + PEER_N_KEYS


def _cparams(sem, vmem_mib, flags=None):
    return pltpu.CompilerParams(dimension_semantics=sem, vmem_limit_bytes=vmem_mib * MIB,
                                flags=flags)


def _inproj_kernel(x_ref, g_ref, w_ref, c_ref, sa_ref, sb_ref, qkv_ref, gate_ref):
    x = x_ref[...]
    ms = jnp.mean(x * x, axis=-1, keepdims=True)
    n = (x * lax.rsqrt(ms + EPS) * g_ref[...]).astype(BF16)
    chunk = 512
    reps = chunk // LANES
    cos = jnp.tile(c_ref[...], (1, reps))
    sin_up = jnp.tile(sa_ref[...], (1, reps))
    sin_dn = jnp.tile(sb_ref[...], (1, reps))
    half = ROPE_DIM // 2
    for j in range(QKV_W // chunk):
        y = jnp.dot(n, w_ref[:, j * chunk:(j + 1) * chunk], preferred_element_type=F32)
        if j < 2 * A_QK_W // chunk:
            y = (y * cos + pltpu.roll(y, half, axis=1) * sin_up
                 + pltpu.roll(y, chunk - half, axis=1) * sin_dn)
            if j < A_QK_W // chunk:
                y = y * (DIFF_HEAD_DIM ** -0.5)
        qkv_ref[:, j * chunk:(j + 1) * chunk] = y.astype(BF16)
    for j in range(GATE_W // chunk):
        c0 = QKV_W + j * chunk
        gate_ref[:, j * chunk:(j + 1) * chunk] = jnp.dot(
            n, w_ref[:, c0:c0 + chunk], preferred_element_type=F32)


def _inproj(x2, g, w_bf, cos_t, sin_up_t, sin_dn_t, seq):
    t = x2.shape[0]
    tm = TM_INPROJ
    nblk = seq // tm
    in_w = w_bf.shape[1]
    tab_spec = pl.BlockSpec((tm, LANES), lambda i: (i % nblk, 0))
    return pl.pallas_call(
        _inproj_kernel,
        grid=(t // tm,),
        in_specs=[
            pl.BlockSpec((tm, D_MODEL), lambda i: (i, 0)),
            pl.BlockSpec((1, D_MODEL), lambda i: (0, 0)),
            pl.BlockSpec((D_MODEL, in_w), lambda i: (0, 0)),
            tab_spec, tab_spec, tab_spec,
        ],
        out_specs=[
            pl.BlockSpec((tm, QKV_W), lambda i: (i, 0)),
            pl.BlockSpec((tm, GATE_W), lambda i: (i, 0)),
        ],
        out_shape=[
            jax.ShapeDtypeStruct((t, QKV_W), BF16),
            jax.ShapeDtypeStruct((t, GATE_W), F32),
        ],
        compiler_params=_cparams(("arbitrary",), 56),
        name="inproj",
    )(x2, g, w_bf, cos_t, sin_up_t, sin_dn_t)


def _diff_attn_kernel(lam_init, q_ref, k_ref, v_ref, lq1_ref, lk1_ref, lq2_ref, lk2_ref,
                      g_ref, o_ref):
    q = q_ref[...]
    k = k_ref[...]
    lane = lax.broadcasted_iota(jnp.int32, q.shape, 1)
    zero = jnp.zeros_like(q)
    q1 = jnp.where(lane < DIFF_HEAD_DIM, q, zero)
    q2 = jnp.where(lane >= DIFF_HEAD_DIM, q, zero)
    nt = (((1,), (1,)), ((), ()))
    s1 = lax.dot_general(q1, k, nt, preferred_element_type=F32)
    s2 = lax.dot_general(q2, k, nt, preferred_element_type=F32)
    e1 = jnp.exp(s1 - jnp.max(s1, axis=-1, keepdims=True))
    e2 = jnp.exp(s2 - jnp.max(s2, axis=-1, keepdims=True))
    l1 = jnp.sum(e1, axis=-1, keepdims=True)
    l2 = jnp.sum(e2, axis=-1, keepdims=True)
    lam = (jnp.exp(jnp.sum(lq1_ref[...] * lk1_ref[...], axis=-1, keepdims=True))
           - jnp.exp(jnp.sum(lq2_ref[...] * lk2_ref[...], axis=-1, keepdims=True))
           + lam_init)
    a = e1 * (1.0 / l1) - e2 * (lam / l2)
    o = jnp.dot(a.astype(BF16), v_ref[...], preferred_element_type=F32)
    ms = jnp.mean(o * o, axis=-1, keepdims=True)
    o = o * lax.rsqrt(ms + EPS) * g_ref[...] * (1.0 - lam_init)
    o_ref[...] = o.astype(o_ref.dtype)


def _diff_attn(qkv3, lq1, lk1, lq2, lk2, subln_g, lam_init):
    b, s, _ = qkv3.shape
    tq = TQ_DIFF
    hw = 2 * DIFF_HEAD_DIM
    k_blk0 = A_QK_W // hw
    v_blk0 = 2 * A_QK_W // hw
    vec = pl.BlockSpec((1, DIFF_HEAD_DIM), lambda bi, h, i: (0, 0))
    return pl.pallas_call(
        functools.partial(_diff_attn_kernel, lam_init),
        grid=(b, N_DIFF_HEADS, s // tq),
        in_specs=[
            pl.BlockSpec((None, tq, hw), lambda bi, h, i: (bi, i, h)),
            pl.BlockSpec((None, s, hw), lambda bi, h, i: (bi, 0, k_blk0 + h)),
            pl.BlockSpec((None, s, DIFF_V_DIM), lambda bi, h, i: (bi, 0, v_blk0 + h)),
            vec, vec, vec, vec,
            pl.BlockSpec((1, DIFF_V_DIM), lambda bi, h, i: (0, 0)),
        ],
        out_specs=pl.BlockSpec((None, tq, DIFF_V_DIM), lambda bi, h, i: (bi, i, h)),
        out_shape=jax.ShapeDtypeStruct((b, s, A_V_W), BF16),
        compiler_params=_cparams(("arbitrary", "arbitrary", "arbitrary"), 48),
        name="diff_attn",
    )(qkv3, qkv3, qkv3, lq1, lk1, lq2, lk2, subln_g)


def _na_window_start(r, rows, kr):
    return jnp.clip(r - kr // 2, 0, rows - kr)


def _na_kernel(rows, kr, q_ref, k_ref, v_ref, bias_ref, o_ref):
    r = pl.program_id(1)
    start = pl.multiple_of(_na_window_start(r, rows, kr) * GRID_W, GRID_W)
    nkeys = kr * GRID_W
    kwin = k_ref[pl.ds(start, nkeys), :]
    vwin = v_ref[pl.ds(start, nkeys), :]
    q = q_ref[...] * (NA_HEAD_DIM ** -0.5)
    nt = (((1,), (1,)), ((), ()))
    pair_w = 2 * NA_HEAD_DIM
    lane = lax.broadcasted_iota(jnp.int32, (GRID_W, pair_w), 1)
    scores = []
    for h in range(N_NA_HEADS):
        sl = slice((h // 2) * pair_w, (h // 2 + 1) * pair_w)
        keep = (lane < NA_HEAD_DIM) if h % 2 == 0 else (lane >= NA_HEAD_DIM)
        qm = jnp.where(keep, q[:, sl], jnp.zeros_like(q[:, sl]))
        s = lax.dot_general(qm, kwin[:, sl], nt, preferred_element_type=F32)
        scores.append(s + bias_ref[h])
    s = jnp.concatenate(scores, axis=0)
    e = jnp.exp(s - jnp.max(s, axis=-1, keepdims=True))
    p = (e * (1.0 / jnp.sum(e, axis=-1, keepdims=True))).astype(BF16)
    for hp in range(N_NA_HEADS // 2):
        sl = slice(hp * pair_w, (hp + 1) * pair_w)
        outs = [jnp.dot(p[(2 * hp + hh) * GRID_W:(2 * hp + hh + 1) * GRID_W], vwin[:, sl],
                        preferred_element_type=F32) for hh in range(2)]
        o_ref[:, sl] = jnp.where(lane < NA_HEAD_DIM, outs[0], outs[1]).astype(o_ref.dtype)


def _na_attn(qkv3, bias_tab, rows, kr):
    b, s, _ = qkv3.shape
    q_blk = (2 * A_QK_W + A_V_W) // B_W
    nkeys = kr * GRID_W

    def bias_idx(bi, r):
        return (0, r - _na_window_start(r, rows, kr), 0, 0)

    return pl.pallas_call(
        functools.partial(_na_kernel, rows, kr),
        grid=(b, rows),
        in_specs=[
            pl.BlockSpec((None, GRID_W, B_W), lambda bi, r: (bi, r, q_blk)),
            pl.BlockSpec((None, s, B_W), lambda bi, r: (bi, 0, q_blk + 1)),
            pl.BlockSpec((None, s, B_W), lambda bi, r: (bi, 0, q_blk + 2)),
            pl.BlockSpec((N_NA_HEADS, None, GRID_W, nkeys), bias_idx),
        ],
        out_specs=pl.BlockSpec((None, GRID_W, B_W), lambda bi, r: (bi, r, 0)),
        out_shape=jax.ShapeDtypeStruct((b, s, B_W), BF16),
        compiler_params=_cparams(("arbitrary", "arbitrary"), 40),
        name="na_attn",
    )(qkv3, qkv3, qkv3, bias_tab)


def _na_bias_table(rpb, rows, kr):
    w = np.arange(GRID_W)[:, None]
    c = np.arange(GRID_W)[None, :]
    cs = np.clip(w - NA_KC // 2, 0, GRID_W - NA_KC)
    valid = (c >= cs) & (c < cs + NA_KC)
    ncol = 2 * NA_KC - 1
    onehot = ((c - w + (NA_KC - 1))[None] == np.arange(ncol)[:, None, None]) & valid[None]
    top = NA_KR_MAX - 1
    rsel = jnp.stack([rpb[:, top - d:top - d + kr, :] for d in range(kr)], axis=1).astype(F32)
    tab = jnp.einsum('hdik,kwc->hdwic', rsel, jnp.asarray(onehot, F32),
                     precision=lax.Precision.HIGHEST)
    tab = jnp.where(valid[None, None, :, None, :], tab, NEG_BIG)
    return tab.reshape(rpb.shape[0], kr, GRID_W, kr * GRID_W)


def _merge_kernel(ya_ref, yb_ref, gate_ref, x_ref, wpa_ref, wpb_ref, wout_ref, g_ref, wq_ref,
                  keys_ref, h_ref, xnt_ref, st_ref):
    ma = jnp.dot(ya_ref[...], wpa_ref[...], preferred_element_type=F32)
    mb = jnp.dot(yb_ref[...], wpb_ref[...], preferred_element_type=F32)
    ga = gate_ref[:, :D_MODEL]
    gb = gate_ref[:, D_MODEL:]
    merged = jax.nn.sigmoid(ga) * ma + jax.nn.sigmoid(gb) * mb
    h = x_ref[...] + jnp.dot(merged.astype(BF16), wout_ref[...], preferred_element_type=F32)
    h_ref[...] = h
    ms = jnp.mean(h * h, axis=-1, keepdims=True)
    xn_f32 = h * lax.rsqrt(ms + EPS) * g_ref[...]
    xn = xn_f32.astype(BF16)
    xnt_ref[...] = xn_f32.T.astype(BF16)
    q = jnp.dot(xn, wq_ref[...], preferred_element_type=F32).astype(BF16)
    nt = (((1,), (1,)), ((), ()))
    half = PEER_QUERY_DIM // 2
    for hp in range(2 * PEER_HEADS):
        st_ref[hp] = lax.dot_general(keys_ref[hp], q[:, hp * half:(hp + 1) * half], nt,
                                     preferred_element_type=F32)


def _merge(ya, yb, gates, x2, wpa, wpb, wout, g_ffn, wq, keys):
    t = x2.shape[0]
    tm = TM_MERGE
    qw = wq.shape[1]
    nlist = keys.shape[0]
    full = lambda shape: pl.BlockSpec(shape, lambda i: (0,) * len(shape))
    return pl.pallas_call(
        _merge_kernel,
        grid=(t // tm,),
        in_specs=[
            pl.BlockSpec((tm, A_V_W), lambda i: (i, 0)),
            pl.BlockSpec((tm, B_W), lambda i: (i, 0)),
            pl.BlockSpec((tm, GATE_W), lambda i: (i, 0)),
            pl.BlockSpec((tm, D_MODEL), lambda i: (i, 0)),
            full((A_V_W, D_MODEL)), full((B_W, D_MODEL)), full((D_MODEL, D_MODEL)),
            full((1, D_MODEL)), full((D_MODEL, qw)), full(keys.shape),
        ],
        out_specs=[
            pl.BlockSpec((tm, D_MODEL), lambda i: (i, 0)),
            pl.BlockSpec((D_MODEL, tm), lambda i: (0, i)),
            pl.BlockSpec((nlist, PEER_N_KEYS, tm), lambda i: (0, 0, i)),
        ],
        out_shape=[
            jax.ShapeDtypeStruct((t, D_MODEL), F32),
            jax.ShapeDtypeStruct((D_MODEL, t), BF16),
            jax.ShapeDtypeStruct((nlist, PEER_N_KEYS, t), F32),
        ],
        compiler_params=_cparams(("arbitrary",), 56),
        name="merge_query",
    )(ya, yb, gates, x2, wpa, wpb, wout, g_ffn, wq, keys)


def _staircase_cells():
    k = PEER_TOPK
    return [(r1, r2) for r1 in range(k) for r2 in range(k // (r1 + 1))]


def _first_max(items):
    level = list(items)
    while len(level) > 1:
        nxt = []
        for i in range(0, len(level) - 1, 2):
            (v0, i0), (v1, i1) = level[i], level[i + 1]
            later = v1 > v0
            nxt.append((jnp.where(later, v1, v0), jnp.where(later, i1, i0)))
        if len(level) % 2:
            nxt.append(level[-1])
        level = nxt
    return level[0]


def _select_kernel(s_ref, cnt_ref, ea_ref, rank2_ref, eb_ref, cur_ref, cand_ref, srt_ref,
                   first_ref, pick_ref):
    k = PEER_TOPK
    nk = s_ref.shape[1]
    tile = s_ref.shape[2:]
    none = jnp.full(tile, -1, jnp.int32)
    cells = _staircase_cells()

    cur_ref[...] = s_ref[...]

    def extract(it, prev):
        winners = []
        for l in range(2):
            items = []
            for key in range(nk):
                c = jnp.where(prev[l] == key, -jnp.inf, cur_ref[l, key])
                cur_ref[l, key] = c
                items.append((c, key))
            m, first = _first_max(items)
            srt_ref[l, pl.ds(it, 1)] = m[None]
            first_ref[l, pl.ds(it, 1)] = first[None]
            winners.append(first)
        return tuple(winners)

    lax.fori_loop(0, k, extract, (none, none))
    as_ = [srt_ref[0, r] for r in range(k)]
    bs_ = [srt_ref[1, r] for r in range(k)]

    for ci, (r1, r2) in enumerate(cells):
        cand_ref[ci] = as_[r1] + bs_[r2]
    top = as_[0] + bs_[0]

    def pick(it, carry):
        prev, z = carry
        items = []
        for ci, (r1, r2) in enumerate(cells):
            pos = r1 * k + r2
            c = jnp.where(prev == pos, -jnp.inf, cand_ref[ci])
            cand_ref[ci] = c
            items.append((c, pos))
        m, first = _first_max(items)
        pick_ref[pl.ds(it, 1)] = first[None]
        return first, z + jnp.exp(m - top)

    _, z = lax.fori_loop(0, k, pick, (none, jnp.zeros(tile, F32)))

    picked_row = [pick_ref[it] // k for it in range(k)]
    counts = []
    for r1 in range(k):
        c = jnp.zeros(tile, F32)
        for it in range(k):
            c = c + jnp.where(picked_row[it] == r1, 1.0, 0.0)
        counts.append(c)

    first_a = [first_ref[0, r] for r in range(k)]
    first_b = [first_ref[1, r] for r in range(k)]
    inv_z = 1.0 / z
    for key in range(nk):
        cnt = jnp.zeros(tile, F32)
        rank = jnp.full(tile, float(nk), F32)
        for r in range(k):
            cnt = jnp.where(first_a[r] == key, counts[r], cnt)
            rank = jnp.where(first_b[r] == key, float(r), rank)
        cnt_ref[key] = cnt
        rank2_ref[key] = rank
        ea_ref[key] = jnp.exp(s_ref[0, key] - as_[0])
        eb_ref[key] = jnp.exp(s_ref[1, key] - bs_[0]) * inv_z


def _select(scores):
    _, nk, heads, t = scores.shape
    k = PEER_TOPK
    tab = pl.BlockSpec((nk, heads, LANES), lambda i: (0, 0, i))
    shape = jax.ShapeDtypeStruct((nk, heads, t), F32)
    return pl.pallas_call(
        _select_kernel,
        grid=(t // LANES,),
        in_specs=[pl.BlockSpec((2, nk, heads, LANES), lambda i: (0, 0, 0, i))],
        out_specs=[tab, tab, tab, tab],
        out_shape=[shape, shape, shape, shape],
        scratch_shapes=[
            pltpu.VMEM((2, nk, heads, LANES), F32),
            pltpu.VMEM((len(_staircase_cells()), heads, LANES), F32),
            pltpu.VMEM((2, k, heads, LANES), F32),
            pltpu.VMEM((2, k, heads, LANES), jnp.int32),
            pltpu.VMEM((k, heads, LANES), jnp.int32),
        ],
        compiler_params=_cparams(("arbitrary",), 32),
        name="peer_select",
    )(scores)


_ERFC_P = 0.3275911
_ERFC_HALF_A = tuple(0.5 * a for a in (0.254829592, -0.284496736, 1.421413741, -1.453152027,
                                       1.061405429))


def _gelu(x):
    z = x * math.sqrt(0.5)
    t = 1.0 / (1.0 + _ERFC_P * jnp.abs(z))
    poly = _ERFC_HALF_A[4]
    for a in _ERFC_HALF_A[3::-1]:
        poly = poly * t + a
    xh = x * (poly * t * jnp.exp(-(z * z)))
    return jnp.where(x >= 0, x - xh, xh)


def _peer_gate_chunk(tc, j_rows, cnt_ref, ea_ref, tab_ref, at_ref, p_ref):
    cols = pl.ds(pl.multiple_of(tc * LANES, LANES), LANES)
    n_k = PEER_N_KEYS // BF16_SUBLANE_ROWS
    acc = [[None] * n_k for _ in j_rows]
    for h in range(PEER_HEADS):
        bcast = lambda ref, j: jnp.broadcast_to(
            ref[j, h:h + 1, cols], (BF16_SUBLANE_ROWS, LANES)).astype(BF16)
        cnt_b = [bcast(cnt_ref, j) for j in j_rows]
        ea_b = [bcast(ea_ref, j) for j in j_rows]
        for k in range(n_k):
            r0 = k * BF16_SUBLANE_ROWS
            r2 = tab_ref[h, tc, r0:r0 + BF16_SUBLANE_ROWS, :]
            ebh = tab_ref[h, tc, TAB_EB_ROW + r0:TAB_EB_ROW + r0 + BF16_SUBLANE_ROWS, :]
            for jj in range(len(j_rows)):
                term = jnp.where(r2 < cnt_b[jj], ebh, 0.0) * ea_b[jj]
                acc[jj][k] = term if acc[jj][k] is None else acc[jj][k] + term
    for jj, j in enumerate(j_rows):
        for k in range(n_k):
            rows = slice(j * PEER_N_KEYS + k * BF16_SUBLANE_ROWS,
                         j * PEER_N_KEYS + (k + 1) * BF16_SUBLANE_ROWS)
            act = at_ref[rows, cols]
            gelu = 0.5 * act * (1.0 + lax.erf(act * math.sqrt(0.5)))
            p_ref[rows, cols] = acc[jj][k] * gelu.astype(BF16)


def _peer_mix_kernel(final_norm, xnt_ref, u_ref, vt_ref, cnt_ref, ea_ref, rank2_ref, eb_ref,
                     h_ref, g_ref, o_ref, at_ref, p_ref, acc_ref, tab_ref):
    e = pl.program_id(1)
    eb, tt = p_ref.shape

    @pl.when(e == 0)
    def _():
        acc_ref[...] = jnp.zeros_like(acc_ref)
        for h in range(PEER_HEADS):
            for tc in range(tt // LANES):
                cols = slice(tc * LANES, (tc + 1) * LANES)
                tab_ref[h, tc, 0:PEER_N_KEYS, :] = rank2_ref[:, h, cols].astype(BF16)
                tab_ref[h, tc, TAB_EB_ROW:TAB_EB_ROW + PEER_N_KEYS, :] = eb_ref[:, h, cols].astype(BF16)

    rows_per_sub = SB_PEER // PEER_N_KEYS
    for sb in range(eb // SB_PEER):
        sub = slice(sb * SB_PEER, (sb + 1) * SB_PEER)
        at_ref[sub, 0:tt] = jnp.dot(u_ref[sub, :], xnt_ref[...], preferred_element_type=F32)
        j_rows = tuple(range(sb * rows_per_sub, (sb + 1) * rows_per_sub))

        def lane_chunk(tc, carry):
            _peer_gate_chunk(tc, j_rows, cnt_ref, ea_ref, tab_ref, at_ref, p_ref)
            return carry

        lax.fori_loop(0, tt // LANES, lane_chunk, 0)
    acc_ref[...] += jnp.dot(vt_ref[...], p_ref[...], preferred_element_type=F32)

    @pl.when(e == pl.num_programs(1) - 1)
    def _():
        y = h_ref[...] + acc_ref[...].T
        if final_norm:
            ms = jnp.mean(y * y, axis=-1, keepdims=True)
            y = y * lax.rsqrt(ms + EPS) * g_ref[...]
        o_ref[...] = y


def _peer_mix(xnt, u_bf, vt_bf, cnt, ea, rank2, eb, h, g_final, final_norm):
    t = xnt.shape[1]
    tt, ebk = TT_PEER, EB_PEER
    n_exp = u_bf.shape[0]
    nk, heads, _ = cnt.shape
    tab = pl.BlockSpec((nk, heads, tt), lambda ti, e: (0, 0, ti))
    row_tab = pl.BlockSpec((ebk // nk, heads, tt), lambda ti, e: (e, 0, ti))
    return pl.pallas_call(
        functools.partial(_peer_mix_kernel, final_norm),
        grid=(t // tt, n_exp // ebk),
        in_specs=[
            pl.BlockSpec((D_MODEL, tt), lambda ti, e: (0, ti)),
            pl.BlockSpec((ebk, D_MODEL), lambda ti, e: (e, 0)),
            pl.BlockSpec((None, D_MODEL, ebk), lambda ti, e: (e, 0, 0)),
            row_tab, row_tab, tab, tab,
            pl.BlockSpec((tt, D_MODEL), lambda ti, e: (ti, 0)),
            pl.BlockSpec((1, D_MODEL), lambda ti, e: (0, 0)),
        ],
        out_specs=pl.BlockSpec((tt, D_MODEL), lambda ti, e: (ti, 0)),
        out_shape=jax.ShapeDtypeStruct((t, D_MODEL), F32),
        scratch_shapes=[
            pltpu.VMEM((ebk, tt + LANES), F32),
            pltpu.VMEM((ebk, tt), BF16),
            pltpu.VMEM((D_MODEL, tt), F32),
            pltpu.VMEM((heads, tt // LANES, TAB_ROWS, LANES), BF16),
        ],
        compiler_params=_cparams(("arbitrary", "arbitrary"), 56),
        name="peer_mix",
    )(xnt, u_bf, vt_bf, cnt, ea, rank2, eb, h, g_final)


def _rope_tables(seq):
    half = ROPE_DIM // 2
    inv_freq = ROPE_THETA ** (-jnp.arange(0, ROPE_DIM, 2, dtype=F32) / ROPE_DIM)
    ang = jnp.arange(seq, dtype=F32)[:, None] * inv_freq[None, :]
    cos, sin = jnp.cos(ang), jnp.sin(ang)
    pad = jnp.zeros((seq, DIFF_HEAD_DIM - ROPE_DIM), F32)
    c64 = jnp.concatenate([cos, cos, pad + 1.0], axis=1)
    up64 = jnp.concatenate([jnp.zeros_like(sin), sin, pad], axis=1)
    dn64 = jnp.concatenate([-sin, jnp.zeros_like(sin), pad], axis=1)
    rep = LANES // DIFF_HEAD_DIM
    return jnp.tile(c64, (1, rep)), jnp.tile(up64, (1, rep)), jnp.tile(dn64, (1, rep))


def kernel(x, w_in, w_proj_a, w_proj_b, w_out, norm_mix, norm_ffn, norm_final, lambda_q1,
           lambda_k1, lambda_q2, lambda_k2, diff_subln, na_rpb, peer_w_query, peer_sub_keys,
           peer_u, peer_v):
    b, s, d = x.shape
    t = b * s
    depth = w_in.shape[0]
    rows = s // GRID_W
    kr = min(NA_KR_MAX, rows)
    cos_t, sin_up_t, sin_dn_t = _rope_tables(s)
    h = x.reshape(t, d)
    row = lambda v: v.reshape(1, -1).astype(F32)
    for l in range(depth):
        lam_init = 0.8 - 0.6 * math.exp(-0.3 * l)
        qkv, gates = _inproj(h, row(norm_mix[l]), w_in[l].astype(BF16), cos_t, sin_up_t,
                             sin_dn_t, s)
        qkv3 = qkv.reshape(b, s, QKV_W)
        ya = _diff_attn(qkv3, row(lambda_q1[l]), row(lambda_k1[l]), row(lambda_q2[l]),
                        row(lambda_k2[l]), row(diff_subln[l]), lam_init)
        yb = _na_attn(qkv3, _na_bias_table(na_rpb[l], rows, kr), rows, kr)
        keys = peer_sub_keys[l].reshape(2 * PEER_HEADS, PEER_N_KEYS, PEER_QUERY_DIM // 2)
        h_mid, xnt, scores_t = _merge(
            ya.reshape(t, A_V_W), yb.reshape(t, B_W), gates, h,
            w_proj_a[l].astype(BF16), w_proj_b[l].astype(BF16), w_out[l].astype(BF16),
            row(norm_ffn[l]), peer_w_query[l].astype(BF16), keys.astype(BF16))
        scores = scores_t.reshape(PEER_HEADS, 2, PEER_N_KEYS, t).transpose(1, 2, 0, 3)
        cnt, ea, rank2, eb = _select(scores)
        vt_blocks = peer_v[l].reshape(-1, EB_PEER, d).transpose(0, 2, 1).astype(BF16)
        h = _peer_mix(xnt, peer_u[l].astype(BF16), vt_blocks, cnt, ea, rank2, eb,
                      h_mid, row(norm_final), final_norm=(l == depth - 1))
    return h.reshape(b, s, d)
```

```python
import functools
import math

import jax
import jax.numpy as jnp
import numpy as np
from jax import lax
from jax.experimental import pallas as pl
from jax.experimental.pallas import tpu as pltpu

F32 = jnp.float32
BF16 = jnp.bfloat16

D_MODEL = 1024
GRID_W = 64
N_DIFF_HEADS = 4
DIFF_HEAD_DIM = 64
DIFF_V_DIM = 2 * DIFF_HEAD_DIM
ROPE_THETA = 500000.0
ROPE_DIM = DIFF_HEAD_DIM // 4
N_NA_HEADS = 8
NA_HEAD_DIM = 64
NA_KR_MAX = 8
NA_KC = 16
A_QK_W = N_DIFF_HEADS * 2 * DIFF_HEAD_DIM
A_V_W = N_DIFF_HEADS * DIFF_V_DIM
B_W = N_NA_HEADS * NA_HEAD_DIM
QKV_W = 2 * A_QK_W + A_V_W + 3 * B_W
GATE_W = 2 * D_MODEL
PEER_HEADS = 8
PEER_N_KEYS = 128
PEER_N_EXPERTS = PEER_N_KEYS * PEER_N_KEYS
PEER_QUERY_DIM = 256
PEER_TOPK = 16
EPS = 1e-6

LANES = 128
NEG_BIG = -1e30
MIB = 1024 * 1024

TM_INPROJ = 512
TQ_DIFF = 256
TM_MERGE = 256
TT_PEER = 512
EB_PEER = 1024
SB_PEER = 512
BF16_SUBLANE_ROWS = 16
TAB_EB_ROW = PEER_N_KEYS + BF16_SUBLANE_ROWS
TAB_ROWS = TAB_EB_ROW + PEER_N_KEYS


def _cparams(sem, vmem_mib, flags=None):
    return pltpu.CompilerParams(dimension_semantics=sem, vmem_limit_bytes=vmem_mib * MIB,
                                flags=flags)


def _inproj_kernel(x_ref, g_ref, w_ref, c_ref, sa_ref, sb_ref, qkv_ref, gate_ref):
    x = x_ref[...]
    ms = jnp.mean(x * x, axis=-1, keepdims=True)
    n = (x * lax.rsqrt(ms + EPS) * g_ref[...]).astype(BF16)
    chunk = 512
    reps = chunk // LANES
    cos = jnp.tile(c_ref[...], (1, reps))
    sin_up = jnp.tile(sa_ref[...], (1, reps))
    sin_dn = jnp.tile(sb_ref[...], (1, reps))
    half = ROPE_DIM // 2
    for j in range(QKV_W // chunk):
        y = jnp.dot(n, w_ref[:, j * chunk:(j + 1) * chunk], preferred_element_type=F32)
        if j < 2 * A_QK_W // chunk:
            y = (y * cos + pltpu.roll(y, half, axis=1) * sin_up
                 + pltpu.roll(y, chunk - half, axis=1) * sin_dn)
            if j < A_QK_W // chunk:
                y = y * (DIFF_HEAD_DIM ** -0.5)
        qkv_ref[:, j * chunk:(j + 1) * chunk] = y.astype(BF16)
    for j in range(GATE_W // chunk):
        c0 = QKV_W + j * chunk
        gate_ref[:, j * chunk:(j + 1) * chunk] = jnp.dot(
            n, w_ref[:, c0:c0 + chunk], preferred_element_type=F32)


def _inproj(x2, g, w_bf, cos_t, sin_up_t, sin_dn_t, seq):
    t = x2.shape[0]
    tm = TM_INPROJ
    nblk = seq // tm
    in_w = w_bf.shape[1]
    tab_spec = pl.BlockSpec((tm, LANES), lambda i: (i % nblk, 0))
    return pl.pallas_call(
        _inproj_kernel,
        grid=(t // tm,),
        in_specs=[
            pl.BlockSpec((tm, D_MODEL), lambda i: (i, 0)),
            pl.BlockSpec((1, D_MODEL), lambda i: (0, 0)),
            pl.BlockSpec((D_MODEL, in_w), lambda i: (0, 0)),
            tab_spec, tab_spec, tab_spec,
        ],
        out_specs=[
            pl.BlockSpec((tm, QKV_W), lambda i: (i, 0)),
            pl.BlockSpec((tm, GATE_W), lambda i: (i, 0)),
        ],
        out_shape=[
            jax.ShapeDtypeStruct((t, QKV_W), BF16),
            jax.ShapeDtypeStruct((t, GATE_W), F32),
        ],
        compiler_params=_cparams(("arbitrary",), 56),
        name="inproj",
    )(x2, g, w_bf, cos_t, sin_up_t, sin_dn_t)


def _diff_attn_kernel(lam_init, q_ref, k_ref, v_ref, lq1_ref, lk1_ref, lq2_ref, lk2_ref,
                      g_ref, o_ref):
    q = q_ref[...]
    k = k_ref[...]
    lane = lax.broadcasted_iota(jnp.int32, q.shape, 1)
    zero = jnp.zeros_like(q)
    q1 = jnp.where(lane < DIFF_HEAD_DIM, q, zero)
    q2 = jnp.where(lane >= DIFF_HEAD_DIM, q, zero)
    nt = (((1,), (1,)), ((), ()))
    s1 = lax.dot_general(q1, k, nt, preferred_element_type=F32)
    s2 = lax.dot_general(q2, k, nt, preferred_element_type=F32)
    e1 = jnp.exp(s1 - jnp.max(s1, axis=-1, keepdims=True))
    e2 = jnp.exp(s2 - jnp.max(s2, axis=-1, keepdims=True))
    l1 = jnp.sum(e1, axis=-1, keepdims=True)
    l2 = jnp.sum(e2, axis=-1, keepdims=True)
    lam = (jnp.exp(jnp.sum(lq1_ref[...] * lk1_ref[...], axis=-1, keepdims=True))
           - jnp.exp(jnp.sum(lq2_ref[...] * lk2_ref[...], axis=-1, keepdims=True))
           + lam_init)
    v = v_ref[...]
    o = (jnp.dot(e1.astype(BF16), v, preferred_element_type=F32) * (1.0 / l1)
         - jnp.dot(e2.astype(BF16), v, preferred_element_type=F32) * (lam / l2))
    ms = jnp.mean(o * o, axis=-1, keepdims=True)
    o = o * lax.rsqrt(ms + EPS) * g_ref[...] * (1.0 - lam_init)
    o_ref[...] = o.astype(o_ref.dtype)


def _diff_attn(qkv3, lq1, lk1, lq2, lk2, subln_g, lam_init):
    b, s, _ = qkv3.shape
    tq = TQ_DIFF
    hw = 2 * DIFF_HEAD_DIM
    k_blk0 = A_QK_W // hw
    v_blk0 = 2 * A_QK_W // hw
    vec = pl.BlockSpec((1, DIFF_HEAD_DIM), lambda bi, h, i: (0, 0))
    return pl.pallas_call(
        functools.partial(_diff_attn_kernel, lam_init),
        grid=(b, N_DIFF_HEADS, s // tq),
        in_specs=[
            pl.BlockSpec((None, tq, hw), lambda bi, h, i: (bi, i, h)),
            pl.BlockSpec((None, s, hw), lambda bi, h, i: (bi, 0, k_blk0 + h)),
            pl.BlockSpec((None, s, DIFF_V_DIM), lambda bi, h, i: (bi, 0, v_blk0 + h)),
            vec, vec, vec, vec,
            pl.BlockSpec((1, DIFF_V_DIM), lambda bi, h, i: (0, 0)),
        ],
        out_specs=pl.BlockSpec((None, tq, DIFF_V_DIM), lambda bi, h, i: (bi, i, h)),
        out_shape=jax.ShapeDtypeStruct((b, s, A_V_W), BF16),
        compiler_params=_cparams(("arbitrary", "arbitrary", "arbitrary"), 48),
        name="diff_attn",
    )(qkv3, qkv3, qkv3, lq1, lk1, lq2, lk2, subln_g)


def _na_window_start(r, rows, kr):
    return jnp.clip(r - kr // 2, 0, rows - kr)


def _na_kernel(rows, kr, q_ref, k_ref, v_ref, bias_ref, o_ref):
    r = pl.program_id(1)
    start = pl.multiple_of(_na_window_start(r, rows, kr) * GRID_W, GRID_W)
    nkeys = kr * GRID_W
    kwin = k_ref[pl.ds(start, nkeys), :]
    vwin = v_ref[pl.ds(start, nkeys), :]
    q = q_ref[...] * (NA_HEAD_DIM ** -0.5)
    nt = (((1,), (1,)), ((), ()))
    pair_w = 2 * NA_HEAD_DIM
    lane = lax.broadcasted_iota(jnp.int32, (GRID_W, pair_w), 1)
    scores = []
    for h in range(N_NA_HEADS):
        sl = slice((h // 2) * pair_w, (h // 2 + 1) * pair_w)
        keep = (lane < NA_HEAD_DIM) if h % 2 == 0 else (lane >= NA_HEAD_DIM)
        qm = jnp.where(keep, q[:, sl], jnp.zeros_like(q[:, sl]))
        s = lax.dot_general(qm, kwin[:, sl], nt, preferred_element_type=F32)
        scores.append(s + bias_ref[h])
    s = jnp.concatenate(scores, axis=0)
    e = jnp.exp(s - jnp.max(s, axis=-1, keepdims=True))
    p = (e * (1.0 / jnp.sum(e, axis=-1, keepdims=True))).astype(BF16)
    for hp in range(N_NA_HEADS // 2):
        sl = slice(hp * pair_w, (hp + 1) * pair_w)
        outs = [jnp.dot(p[(2 * hp + hh) * GRID_W:(2 * hp + hh + 1) * GRID_W], vwin[:, sl],
                        preferred_element_type=F32) for hh in range(2)]
        o_ref[:, sl] = jnp.where(lane < NA_HEAD_DIM, outs[0], outs[1]).astype(o_ref.dtype)


def _na_attn(qkv3, bias_tab, rows, kr):
    b, s, _ = qkv3.shape
    q_blk = (2 * A_QK_W + A_V_W) // B_W
    nkeys = kr * GRID_W

    def bias_idx(bi, r):
        return (0, r - _na_window_start(r, rows, kr), 0, 0)

    return pl.pallas_call(
        functools.partial(_na_kernel, rows, kr),
        grid=(b, rows),
        in_specs=[
            pl.BlockSpec((None, GRID_W, B_W), lambda bi, r: (bi, r, q_blk)),
            pl.BlockSpec((None, s, B_W), lambda bi, r: (bi, 0, q_blk + 1)),
            pl.BlockSpec((None, s, B_W), lambda bi, r: (bi, 0, q_blk + 2)),
            pl.BlockSpec((N_NA_HEADS, None, GRID_W, nkeys), bias_idx),
        ],
        out_specs=pl.BlockSpec((None, GRID_W, B_W), lambda bi, r: (bi, r, 0)),
        out_shape=jax.ShapeDtypeStruct((b, s, B_W), BF16),
        compiler_params=_cparams(("arbitrary", "arbitrary"), 40),
        name="na_attn",
    )(qkv3, qkv3, qkv3, bias_tab)


def _na_bias_table(rpb, rows, kr):
    w = np.arange(GRID_W)[:, None]
    c = np.arange(GRID_W)[None, :]
    cs = np.clip(w - NA_KC // 2, 0, GRID_W - NA_KC)
    valid = (c >= cs) & (c < cs + NA_KC)
    ncol = 2 * NA_KC - 1
    onehot = ((c - w + (NA_KC - 1))[None] == np.arange(ncol)[:, None, None]) & valid[None]
    top = NA_KR_MAX - 1
    rsel = jnp.stack([rpb[:, top - d:top - d + kr, :] for d in range(kr)], axis=1).astype(F32)
    tab = jnp.einsum('hdik,kwc->hdwic', rsel, jnp.asarray(onehot, F32),
                     precision=lax.Precision.HIGHEST)
    tab = jnp.where(valid[None, None, :, None, :], tab, NEG_BIG)
    return tab.reshape(rpb.shape[0], kr, GRID_W, kr * GRID_W)


def _merge_kernel(ya_ref, yb_ref, gate_ref, x_ref, wpa_ref, wpb_ref, wout_ref, g_ref, wq_ref,
                  keys_ref, h_ref, xnt_ref, st_ref):
    ma = jnp.dot(ya_ref[...], wpa_ref[...], preferred_element_type=F32)
    mb = jnp.dot(yb_ref[...], wpb_ref[...], preferred_element_type=F32)
    ga = gate_ref[:, :D_MODEL]
    gb = gate_ref[:, D_MODEL:]
    merged = jax.nn.sigmoid(ga) * ma + jax.nn.sigmoid(gb) * mb
    h = x_ref[...] + jnp.dot(merged.astype(BF16), wout_ref[...], preferred_element_type=F32)
    h_ref[...] = h
    ms = jnp.mean(h * h, axis=-1, keepdims=True)
    xn_f32 = h * lax.rsqrt(ms + EPS) * g_ref[...]
    xn = xn_f32.astype(BF16)
    xnt_ref[...] = xn_f32.T.astype(BF16)
    q = jnp.dot(xn, wq_ref[...], preferred_element_type=F32).astype(BF16)
    nt = (((1,), (1,)), ((), ()))
    half = PEER_QUERY_DIM // 2
    for hp in range(2 * PEER_HEADS):
        st_ref[hp % 2, :, hp // 2, :] = lax.dot_general(
            keys_ref[hp], q[:, hp * half:(hp + 1) * half], nt, preferred_element_type=F32)


def _merge(ya, yb, gates, x2, wpa, wpb, wout, g_ffn, wq, keys):
    t = x2.shape[0]
    tm = TM_MERGE
    qw = wq.shape[1]
    nlist = keys.shape[0]
    full = lambda shape: pl.BlockSpec(shape, lambda i: (0,) * len(shape))
    return pl.pallas_call(
        _merge_kernel,
        grid=(t // tm,),
        in_specs=[
            pl.BlockSpec((tm, A_V_W), lambda i: (i, 0)),
            pl.BlockSpec((tm, B_W), lambda i: (i, 0)),
            pl.BlockSpec((tm, GATE_W), lambda i: (i, 0)),
            pl.BlockSpec((tm, D_MODEL), lambda i: (i, 0)),
            full((A_V_W, D_MODEL)), full((B_W, D_MODEL)), full((D_MODEL, D_MODEL)),
            full((1, D_MODEL)), full((D_MODEL, qw)), full(keys.shape),
        ],
        out_specs=[
            pl.BlockSpec((tm, D_MODEL), lambda i: (i, 0)),
            pl.BlockSpec((D_MODEL, tm), lambda i: (0, i)),
            pl.BlockSpec((2, PEER_N_KEYS, nlist // 2, tm), lambda i: (0, 0, 0, i)),
        ],
        out_shape=[
            jax.ShapeDtypeStruct((t, D_MODEL), F32),
            jax.ShapeDtypeStruct((D_MODEL, t), BF16),
            jax.ShapeDtypeStruct((2, PEER_N_KEYS, nlist // 2, t), F32),
        ],
        compiler_params=_cparams(("arbitrary",), 56),
        name="merge_query",
    )(ya, yb, gates, x2, wpa, wpb, wout, g_ffn, wq, keys)


def _staircase_cells():
    k = PEER_TOPK
    return [(r1, r2) for r1 in range(k) for r2 in range(k // (r1 + 1))]


def _first_max(items):
    level = list(items)
    while len(level) > 1:
        nxt = []
        for i in range(0, len(level) - 1, 2):
            (v0, i0), (v1, i1) = level[i], level[i + 1]
            later = v1 > v0
            nxt.append((jnp.where(later, v1, v0), jnp.where(later, i1, i0)))
        if len(level) % 2:
            nxt.append(level[-1])
        level = nxt
    return level[0]


def _select_kernel(s_ref, cnt_ref, ea_ref, rank2_ref, eb_ref, cur_ref, cand_ref, srt_ref,
                   first_ref, pick_ref):
    k = PEER_TOPK
    nk = s_ref.shape[1]
    tile = s_ref.shape[2:]
    none = jnp.full(tile, -1, jnp.int32)
    cells = _staircase_cells()

    cur_ref[...] = s_ref[...]

    def extract(it, prev):
        winners = []
        for l in range(2):
            items = []
            for key in range(nk):
                c = jnp.where(prev[l] == key, -jnp.inf, cur_ref[l, key])
                cur_ref[l, key] = c
                items.append((c, key))
            m, first = _first_max(items)
            srt_ref[l, pl.ds(it, 1)] = m[None]
            first_ref[l, pl.ds(it, 1)] = first[None]
            winners.append(first)
        return tuple(winners)

    lax.fori_loop(0, k, extract, (none, none))
    as_ = [srt_ref[0, r] for r in range(k)]
    bs_ = [srt_ref[1, r] for r in range(k)]

    for ci, (r1, r2) in enumerate(cells):
        cand_ref[ci] = as_[r1] + bs_[r2]
    top = as_[0] + bs_[0]

    def pick(it, carry):
        prev, z = carry
        items = []
        for ci, (r1, r2) in enumerate(cells):
            pos = r1 * k + r2
            c = jnp.where(prev == pos, -jnp.inf, cand_ref[ci])
            cand_ref[ci] = c
            items.append((c, pos))
        m, first = _first_max(items)
        pick_ref[pl.ds(it, 1)] = first[None]
        return first, z + jnp.exp(m - top)

    _, z = lax.fori_loop(0, k, pick, (none, jnp.zeros(tile, F32)))

    picked_row = [pick_ref[it] // k for it in range(k)]
    counts = []
    for r1 in range(k):
        c = jnp.zeros(tile, F32)
        for it in range(k):
            c = c + jnp.where(picked_row[it] == r1, 1.0, 0.0)
        counts.append(c)

    first_a = [first_ref[0, r] for r in range(k)]
    first_b = [first_ref[1, r] for r in range(k)]
    inv_z = 1.0 / z
    for key in range(nk):
        cnt = jnp.zeros(tile, F32)
        rank = jnp.full(tile, float(nk), F32)
        for r in range(k):
            cnt = jnp.where(first_a[r] == key, counts[r], cnt)
            rank = jnp.where(first_b[r] == key, float(r), rank)
        cnt_ref[key] = cnt
        rank2_ref[:, key, :] = rank
        ea_ref[key] = jnp.exp(s_ref[0, key] - as_[0])
        eb_ref[:, key, :] = jnp.exp(s_ref[1, key] - bs_[0]) * inv_z


def _select(scores):
    _, nk, heads, t = scores.shape
    k = PEER_TOPK
    tab = pl.BlockSpec((nk, heads, LANES), lambda i: (0, 0, i))
    shape = jax.ShapeDtypeStruct((nk, heads, t), F32)
    htab = pl.BlockSpec((heads, nk, LANES), lambda i: (0, 0, i))
    hshape = jax.ShapeDtypeStruct((heads, nk, t), F32)
    return pl.pallas_call(
        _select_kernel,
        grid=(t // LANES,),
        in_specs=[pl.BlockSpec((2, nk, heads, LANES), lambda i: (0, 0, 0, i))],
        out_specs=[tab, tab, htab, htab],
        out_shape=[shape, shape, hshape, hshape],
        scratch_shapes=[
            pltpu.VMEM((2, nk, heads, LANES), F32),
            pltpu.VMEM((len(_staircase_cells()), heads, LANES), F32),
            pltpu.VMEM((2, k, heads, LANES), F32),
            pltpu.VMEM((2, k, heads, LANES), jnp.int32),
            pltpu.VMEM((k, heads, LANES), jnp.int32),
        ],
        compiler_params=_cparams(("arbitrary",), 32),
        name="peer_select",
    )(scores)


_ERFC_P = 0.3275911
_ERFC_HALF_A = tuple(0.5 * a for a in (0.254829592, -0.284496736, 1.421413741, -1.453152027,
                                       1.061405429))


def _gelu(x):
    z = x * math.sqrt(0.5)
    t = 1.0 / (1.0 + _ERFC_P * jnp.abs(z))
    poly = _ERFC_HALF_A[4]
    for a in _ERFC_HALF_A[3::-1]:
        poly = poly * t + a
    xh = x * (poly * t * jnp.exp(-(z * z)))
    return jnp.where(x >= 0, x - xh, xh)


def _peer_gate_chunk(tc, j_rows, cnt_ref, ea_ref, tab_ref, at_ref, p_ref):
    cols = pl.ds(pl.multiple_of(tc * LANES, LANES), LANES)
    n_k = PEER_N_KEYS // BF16_SUBLANE_ROWS
    acc = [[None] * n_k for _ in j_rows]
    for h in range(PEER_HEADS):
        bcast = lambda ref, j: jnp.broadcast_to(
            ref[j, h:h + 1, cols], (BF16_SUBLANE_ROWS, LANES)).astype(BF16)
        cnt_b = [bcast(cnt_ref, j) for j in j_rows]
        ea_b = [bcast(ea_ref, j) for j in j_rows]
        for k in range(n_k):
            r0 = k * BF16_SUBLANE_ROWS
            r2 = tab_ref[h, tc, r0:r0 + BF16_SUBLANE_ROWS, :]
            ebh = tab_ref[h, tc, TAB_EB_ROW + r0:TAB_EB_ROW + r0 + BF16_SUBLANE_ROWS, :]
            for jj in range(len(j_rows)):
                term = jnp.where(r2 < cnt_b[jj], ebh, 0.0) * ea_b[jj]
                acc[jj][k] = term if acc[jj][k] is None else acc[jj][k] + term
    for jj, j in enumerate(j_rows):
        for k in range(n_k):
            rows = slice(j * PEER_N_KEYS + k * BF16_SUBLANE_ROWS,
                         j * PEER_N_KEYS + (k + 1) * BF16_SUBLANE_ROWS)
            act = at_ref[rows, cols]
            gelu = 0.5 * act * (1.0 + lax.erf(act * math.sqrt(0.5)))
            p_ref[rows, cols] = acc[jj][k] * gelu.astype(BF16)


def _peer_mix_kernel(final_norm, xnt_ref, u_ref, vt_ref, cnt_ref, ea_ref, rank2_ref, eb_ref,
                     h_ref, g_ref, o_ref, at_ref, p_ref, acc_ref, tab_ref):
    e = pl.program_id(1)
    eb, tt = p_ref.shape

    @pl.when(e == 0)
    def _():
        acc_ref[...] = jnp.zeros_like(acc_ref)
        for h in range(PEER_HEADS):
            for tc in range(tt // LANES):
                cols = slice(tc * LANES, (tc + 1) * LANES)
                tab_ref[h, tc, 0:PEER_N_KEYS, :] = rank2_ref[h, :, cols].astype(BF16)
                tab_ref[h, tc, TAB_EB_ROW:TAB_EB_ROW + PEER_N_KEYS, :] = eb_ref[h, :, cols].astype(BF16)

    rows_per_sub = SB_PEER // PEER_N_KEYS
    for sb in range(eb // SB_PEER):
        sub = slice(sb * SB_PEER, (sb + 1) * SB_PEER)
        at_ref[sub, 0:tt] = jnp.dot(u_ref[sub, :], xnt_ref[...], preferred_element_type=F32)
        j_rows = tuple(range(sb * rows_per_sub, (sb + 1) * rows_per_sub))

        def lane_chunk(tc, carry):
            _peer_gate_chunk(tc, j_rows, cnt_ref, ea_ref, tab_ref, at_ref, p_ref)
            return carry

        lax.fori_loop(0, tt // LANES, lane_chunk, 0)
    acc_ref[...] += jnp.dot(vt_ref[...], p_ref[...], preferred_element_type=F32)

    @pl.when(e == pl.num_programs(1) - 1)
    def _():
        y = h_ref[...] + acc_ref[...].T
        if final_norm:
            ms = jnp.mean(y * y, axis=-1, keepdims=True)
            y = y * lax.rsqrt(ms + EPS) * g_ref[...]
        o_ref[...] = y


def _peer_mix(xnt, u_bf, vt_bf, cnt, ea, rank2, eb, h, g_final, final_norm):
    t = xnt.shape[1]
    tt, ebk = TT_PEER, EB_PEER
    n_exp = u_bf.shape[0]
    nk, heads, _ = cnt.shape
    tab = pl.BlockSpec((nk, heads, tt), lambda ti, e: (0, 0, ti))
    row_tab = pl.BlockSpec((ebk // nk, heads, tt), lambda ti, e: (e, 0, ti))
    htab = pl.BlockSpec((heads, nk, tt), lambda ti, e: (0, 0, ti))
    return pl.pallas_call(
        functools.partial(_peer_mix_kernel, final_norm),
        grid=(t // tt, n_exp // ebk),
        in_specs=[
            pl.BlockSpec((D_MODEL, tt), lambda ti, e: (0, ti)),
            pl.BlockSpec((ebk, D_MODEL), lambda ti, e: (e, 0)),
            pl.BlockSpec((None, D_MODEL, ebk), lambda ti, e: (e, 0, 0)),
            row_tab, row_tab, htab, htab,
            pl.BlockSpec((tt, D_MODEL), lambda ti, e: (ti, 0)),
            pl.BlockSpec((1, D_MODEL), lambda ti, e: (0, 0)),
        ],
        out_specs=pl.BlockSpec((tt, D_MODEL), lambda ti, e: (ti, 0)),
        out_shape=jax.ShapeDtypeStruct((t, D_MODEL), F32),
        scratch_shapes=[
            pltpu.VMEM((ebk, tt + LANES), F32),
            pltpu.VMEM((ebk, tt), BF16),
            pltpu.VMEM((D_MODEL, tt), F32),
            pltpu.VMEM((heads, tt // LANES, TAB_ROWS, LANES), BF16),
        ],
        compiler_params=_cparams(("arbitrary", "arbitrary"), 56),
        name="peer_mix",
    )(xnt, u_bf, vt_bf, cnt, ea, rank2, eb, h, g_final)


def _rope_tables(seq):
    half = ROPE_DIM // 2
    inv_freq = ROPE_THETA ** (-jnp.arange(0, ROPE_DIM, 2, dtype=F32) / ROPE_DIM)
    ang = jnp.arange(seq, dtype=F32)[:, None] * inv_freq[None, :]
    cos, sin = jnp.cos(ang), jnp.sin(ang)
    pad = jnp.zeros((seq, DIFF_HEAD_DIM - ROPE_DIM), F32)
    c64 = jnp.concatenate([cos, cos, pad + 1.0], axis=1)
    up64 = jnp.concatenate([jnp.zeros_like(sin), sin, pad], axis=1)
    dn64 = jnp.concatenate([-sin, jnp.zeros_like(sin), pad], axis=1)
    rep = LANES // DIFF_HEAD_DIM
    return jnp.tile(c64, (1, rep)), jnp.tile(up64, (1, rep)), jnp.tile(dn64, (1, rep))


def kernel(x, w_in, w_proj_a, w_proj_b, w_out, norm_mix, norm_ffn, norm_final, lambda_q1,
           lambda_k1, lambda_q2, lambda_k2, diff_subln, na_rpb, peer_w_query, peer_sub_keys,
           peer_u, peer_v):
    b, s, d = x.shape
    t = b * s
    depth = w_in.shape[0]
    rows = s // GRID_W
    kr = min(NA_KR_MAX, rows)
    cos_t, sin_up_t, sin_dn_t = _rope_tables(s)
    h = x.reshape(t, d)
    row = lambda v: v.reshape(1, -1).astype(F32)
    for l in range(depth):
        lam_init = 0.8 - 0.6 * math.exp(-0.3 * l)
        qkv, gates = _inproj(h, row(norm_mix[l]), w_in[l].astype(BF16), cos_t, sin_up_t,
                             sin_dn_t, s)
        qkv3 = qkv.reshape(b, s, QKV_W)
        ya = _diff_attn(qkv3, row(lambda_q1[l]), row(lambda_k1[l]), row(lambda_q2[l]),
                        row(lambda_k2[l]), row(diff_subln[l]), lam_init)
        yb = _na_attn(qkv3, _na_bias_table(na_rpb[l], rows, kr), rows, kr)
        keys = peer_sub_keys[l].reshape(2 * PEER_HEADS, PEER_N_KEYS, PEER_QUERY_DIM // 2)
        h_mid, xnt, scores_t = _merge(
            ya.reshape(t, A_V_W), yb.reshape(t, B_W), gates, h,
            w_proj_a[l].astype(BF16), w_proj_b[l].astype(BF16), w_out[l].astype(BF16),
            row(norm_ffn[l]), peer_w_query[l].astype(BF16), keys.astype(BF16))
        cnt, ea, rank2, eb = _select(scores_t)
        vt_blocks = peer_v[l].reshape(-1, EB_PEER, d).transpose(0, 2, 1).astype(BF16)
        h = _peer_mix(xnt, peer_u[l].astype(BF16), vt_blocks, cnt, ea, rank2, eb,
                      h_mid, row(norm_final), final_norm=(l == depth - 1))
    return h.reshape(b, s, d)
```

```python
import functools
import math

import jax
import jax.numpy as jnp
import numpy as np
from jax import lax
from jax.experimental import pallas as pl
from jax.experimental.pallas import tpu as pltpu

F32 = jnp.float32
BF16 = jnp.bfloat16

D_MODEL = 1024
GRID_W = 64
N_DIFF_HEADS = 4
DIFF_HEAD_DIM = 64
DIFF_V_DIM = 2 * DIFF_HEAD_DIM
ROPE_THETA = 500000.0
ROPE_DIM = DIFF_HEAD_DIM // 4
N_NA_HEADS = 8
NA_HEAD_DIM = 64
NA_KR_MAX = 8
NA_KC = 16
A_QK_W = N_DIFF_HEADS * 2 * DIFF_HEAD_DIM
A_V_W = N_DIFF_HEADS * DIFF_V_DIM
B_W = N_NA_HEADS * NA_HEAD_DIM
QKV_W = 2 * A_QK_W + A_V_W + 3 * B_W
GATE_W = 2 * D_MODEL
PEER_HEADS = 8
PEER_N_KEYS = 128
PEER_N_EXPERTS = PEER_N_KEYS * PEER_N_KEYS
PEER_QUERY_DIM = 256
PEER_TOPK = 16
EPS = 1e-6

LANES = 128
NEG_BIG = -1e30
MIB = 1024 * 1024

TM_INPROJ = 512
TQ_DIFF = 256
TM_MERGE = 256
TT_PEER = 512
EB_PEER = 2048
SB_PEER = 512
GATE_ROWS = 2
BF16_SUBLANE_ROWS = 16
TAB_EB_ROW = PEER_N_KEYS + BF16_SUBLANE_ROWS
TAB_ROWS = TAB_EB_ROW + PEER_N_KEYS


def _cparams(sem, vmem_mib, flags=None):
    return pltpu.CompilerParams(dimension_semantics=sem, vmem_limit_bytes=vmem_mib * MIB,
                                flags=flags)


def _inproj_kernel(x_ref, g_ref, w_ref, c_ref, sa_ref, sb_ref, qkv_ref, gate_ref):
    x = x_ref[...]
    ms = jnp.mean(x * x, axis=-1, keepdims=True)
    n = (x * lax.rsqrt(ms + EPS) * g_ref[...]).astype(BF16)
    chunk = 512
    reps = chunk // LANES
    cos = jnp.tile(c_ref[...], (1, reps))
    sin_up = jnp.tile(sa_ref[...], (1, reps))
    sin_dn = jnp.tile(sb_ref[...], (1, reps))
    half = ROPE_DIM // 2
    for j in range(QKV_W // chunk):
        y = jnp.dot(n, w_ref[:, j * chunk:(j + 1) * chunk], preferred_element_type=F32)
        if j < 2 * A_QK_W // chunk:
            y = (y * cos + pltpu.roll(y, half, axis=1) * sin_up
                 + pltpu.roll(y, chunk - half, axis=1) * sin_dn)
            if j < A_QK_W // chunk:
                y = y * (DIFF_HEAD_DIM ** -0.5)
        qkv_ref[:, j * chunk:(j + 1) * chunk] = y.astype(BF16)
    for j in range(GATE_W // chunk):
        c0 = QKV_W + j * chunk
        gate_ref[:, j * chunk:(j + 1) * chunk] = jnp.dot(
            n, w_ref[:, c0:c0 + chunk], preferred_element_type=F32)


def _inproj(x2, g, w_bf, cos_t, sin_up_t, sin_dn_t, seq):
    t = x2.shape[0]
    tm = TM_INPROJ
    nblk = seq // tm
    in_w = w_bf.shape[1]
    tab_spec = pl.BlockSpec((tm, LANES), lambda i: (i % nblk, 0))
    return pl.pallas_call(
        _inproj_kernel,
        grid=(t // tm,),
        in_specs=[
            pl.BlockSpec((tm, D_MODEL), lambda i: (i, 0)),
            pl.BlockSpec((1, D_MODEL), lambda i: (0, 0)),
            pl.BlockSpec((D_MODEL, in_w), lambda i: (0, 0)),
            tab_spec, tab_spec, tab_spec,
        ],
        out_specs=[
            pl.BlockSpec((tm, QKV_W), lambda i: (i, 0)),
            pl.BlockSpec((tm, GATE_W), lambda i: (i, 0)),
        ],
        out_shape=[
            jax.ShapeDtypeStruct((t, QKV_W), BF16),
            jax.ShapeDtypeStruct((t, GATE_W), F32),
        ],
        compiler_params=_cparams(("arbitrary",), 56),
        name="inproj",
    )(x2, g, w_bf, cos_t, sin_up_t, sin_dn_t)


def _diff_attn_kernel(lam_init, q_ref, k_ref, v_ref, lq1_ref, lk1_ref, lq2_ref, lk2_ref,
                      g_ref, o_ref):
    q = q_ref[...]
    k = k_ref[...]
    lane = lax.broadcasted_iota(jnp.int32, q.shape, 1)
    zero = jnp.zeros_like(q)
    q1 = jnp.where(lane < DIFF_HEAD_DIM, q, zero)
    q2 = jnp.where(lane >= DIFF_HEAD_DIM, q, zero)
    nt = (((1,), (1,)), ((), ()))
    s1 = lax.dot_general(q1, k, nt, preferred_element_type=F32)
    s2 = lax.dot_general(q2, k, nt, preferred_element_type=F32)
    e1 = jnp.exp(s1 - jnp.max(s1, axis=-1, keepdims=True))
    e2 = jnp.exp(s2 - jnp.max(s2, axis=-1, keepdims=True))
    l1 = jnp.sum(e1, axis=-1, keepdims=True)
    l2 = jnp.sum(e2, axis=-1, keepdims=True)
    lam = (jnp.exp(jnp.sum(lq1_ref[...] * lk1_ref[...], axis=-1, keepdims=True))
           - jnp.exp(jnp.sum(lq2_ref[...] * lk2_ref[...], axis=-1, keepdims=True))
           + lam_init)
    v = v_ref[...]
    o = (jnp.dot(e1.astype(BF16), v, preferred_element_type=F32) * (1.0 / l1)
         - jnp.dot(e2.astype(BF16), v, preferred_element_type=F32) * (lam / l2))
    ms = jnp.mean(o * o, axis=-1, keepdims=True)
    o = o * lax.rsqrt(ms + EPS) * g_ref[...] * (1.0 - lam_init)
    o_ref[...] = o.astype(o_ref.dtype)


def _diff_attn(qkv3, lq1, lk1, lq2, lk2, subln_g, lam_init):
    b, s, _ = qkv3.shape
    tq = TQ_DIFF
    hw = 2 * DIFF_HEAD_DIM
    k_blk0 = A_QK_W // hw
    v_blk0 = 2 * A_QK_W // hw
    vec = pl.BlockSpec((1, DIFF_HEAD_DIM), lambda bi, h, i: (0, 0))
    return pl.pallas_call(
        functools.partial(_diff_attn_kernel, lam_init),
        grid=(b, N_DIFF_HEADS, s // tq),
        in_specs=[
            pl.BlockSpec((None, tq, hw), lambda bi, h, i: (bi, i, h)),
            pl.BlockSpec((None, s, hw), lambda bi, h, i: (bi, 0, k_blk0 + h)),
            pl.BlockSpec((None, s, DIFF_V_DIM), lambda bi, h, i: (bi, 0, v_blk0 + h)),
            vec, vec, vec, vec,
            pl.BlockSpec((1, DIFF_V_DIM), lambda bi, h, i: (0, 0)),
        ],
        out_specs=pl.BlockSpec((None, tq, DIFF_V_DIM), lambda bi, h, i: (bi, i, h)),
        out_shape=jax.ShapeDtypeStruct((b, s, A_V_W), BF16),
        compiler_params=_cparams(("arbitrary", "arbitrary", "arbitrary"), 48),
        name="diff_attn",
    )(qkv3, qkv3, qkv3, lq1, lk1, lq2, lk2, subln_g)


def _na_window_start(r, rows, kr):
    return jnp.clip(r - kr // 2, 0, rows - kr)


def _na_kernel(rows, kr, q_ref, k_ref, v_ref, bias_ref, o_ref):
    r = pl.program_id(1)
    start = pl.multiple_of(_na_window_start(r, rows, kr) * GRID_W, GRID_W)
    nkeys = kr * GRID_W
    kwin = k_ref[pl.ds(start, nkeys), :]
    vwin = v_ref[pl.ds(start, nkeys), :]
    q = q_ref[...] * (NA_HEAD_DIM ** -0.5)
    nt = (((1,), (1,)), ((), ()))
    pair_w = 2 * NA_HEAD_DIM
    lane = lax.broadcasted_iota(jnp.int32, (GRID_W, pair_w), 1)
    scores = []
    for h in range(N_NA_HEADS):
        sl = slice((h // 2) * pair_w, (h // 2 + 1) * pair_w)
        keep = (lane < NA_HEAD_DIM) if h % 2 == 0 else (lane >= NA_HEAD_DIM)
        qm = jnp.where(keep, q[:, sl], jnp.zeros_like(q[:, sl]))
        s = lax.dot_general(qm, kwin[:, sl], nt, preferred_element_type=F32)
        scores.append(s + bias_ref[h])
    s = jnp.concatenate(scores, axis=0)
    e = jnp.exp(s - jnp.max(s, axis=-1, keepdims=True))
    p = (e * (1.0 / jnp.sum(e, axis=-1, keepdims=True))).astype(BF16)
    for hp in range(N_NA_HEADS // 2):
        sl = slice(hp * pair_w, (hp + 1) * pair_w)
        outs = [jnp.dot(p[(2 * hp + hh) * GRID_W:(2 * hp + hh + 1) * GRID_W], vwin[:, sl],
                        preferred_element_type=F32) for hh in range(2)]
        o_ref[:, sl] = jnp.where(lane < NA_HEAD_DIM, outs[0], outs[1]).astype(o_ref.dtype)


def _na_attn(qkv3, bias_tab, rows, kr):
    b, s, _ = qkv3.shape
    q_blk = (2 * A_QK_W + A_V_W) // B_W
    nkeys = kr * GRID_W

    def bias_idx(bi, r):
        return (0, r - _na_window_start(r, rows, kr), 0, 0)

    return pl.pallas_call(
        functools.partial(_na_kernel, rows, kr),
        grid=(b, rows),
        in_specs=[
            pl.BlockSpec((None, GRID_W, B_W), lambda bi, r: (bi, r, q_blk)),
            pl.BlockSpec((None, s, B_W), lambda bi, r: (bi, 0, q_blk + 1)),
            pl.BlockSpec((None, s, B_W), lambda bi, r: (bi, 0, q_blk + 2)),
            pl.BlockSpec((N_NA_HEADS, None, GRID_W, nkeys), bias_idx),
        ],
        out_specs=pl.BlockSpec((None, GRID_W, B_W), lambda bi, r: (bi, r, 0)),
        out_shape=jax.ShapeDtypeStruct((b, s, B_W), BF16),
        compiler_params=_cparams(("arbitrary", "arbitrary"), 40),
        name="na_attn",
    )(qkv3, qkv3, qkv3, bias_tab)


def _na_bias_table(rpb, rows, kr):
    w = np.arange(GRID_W)[:, None]
    c = np.arange(GRID_W)[None, :]
    cs = np.clip(w - NA_KC // 2, 0, GRID_W - NA_KC)
    valid = (c >= cs) & (c < cs + NA_KC)
    ncol = 2 * NA_KC - 1
    onehot = ((c - w + (NA_KC - 1))[None] == np.arange(ncol)[:, None, None]) & valid[None]
    top = NA_KR_MAX - 1
    rsel = jnp.stack([rpb[:, top - d:top - d + kr, :] for d in range(kr)], axis=1).astype(F32)
    tab = jnp.einsum('hdik,kwc->hdwic', rsel, jnp.asarray(onehot, F32),
                     precision=lax.Precision.HIGHEST)
    tab = jnp.where(valid[None, None, :, None, :], tab, NEG_BIG)
    return tab.reshape(rpb.shape[0], kr, GRID_W, kr * GRID_W)


def _merge_kernel(ya_ref, yb_ref, gate_ref, x_ref, wpa_ref, wpb_ref, wout_ref, g_ref, wq_ref,
                  keys_ref, h_ref, xnt_ref, st_ref):
    ma = jnp.dot(ya_ref[...], wpa_ref[...], preferred_element_type=F32)
    mb = jnp.dot(yb_ref[...], wpb_ref[...], preferred_element_type=F32)
    ga = gate_ref[:, :D_MODEL]
    gb = gate_ref[:, D_MODEL:]
    merged = jax.nn.sigmoid(ga) * ma + jax.nn.sigmoid(gb) * mb
    h = x_ref[...] + jnp.dot(merged.astype(BF16), wout_ref[...], preferred_element_type=F32)
    h_ref[...] = h
    ms = jnp.mean(h * h, axis=-1, keepdims=True)
    xn_f32 = h * lax.rsqrt(ms + EPS) * g_ref[...]
    xn = xn_f32.astype(BF16)
    xnt_ref[...] = xn_f32.T.astype(BF16)
    q = jnp.dot(xn, wq_ref[...], preferred_element_type=F32).astype(BF16)
    nt = (((1,), (1,)), ((), ()))
    half = PEER_QUERY_DIM // 2
    for hp in range(2 * PEER_HEADS):
        st_ref[hp % 2, :, hp // 2, :] = lax.dot_general(
            keys_ref[hp], q[:, hp * half:(hp + 1) * half], nt, preferred_element_type=F32)


def _merge(ya, yb, gates, x2, wpa, wpb, wout, g_ffn, wq, keys):
    t = x2.shape[0]
    tm = TM_MERGE
    qw = wq.shape[1]
    nlist = keys.shape[0]
    full = lambda shape: pl.BlockSpec(shape, lambda i: (0,) * len(shape))
    return pl.pallas_call(
        _merge_kernel,
        grid=(t // tm,),
        in_specs=[
            pl.BlockSpec((tm, A_V_W), lambda i: (i, 0)),
            pl.BlockSpec((tm, B_W), lambda i: (i, 0)),
            pl.BlockSpec((tm, GATE_W), lambda i: (i, 0)),
            pl.BlockSpec((tm, D_MODEL), lambda i: (i, 0)),
            full((A_V_W, D_MODEL)), full((B_W, D_MODEL)), full((D_MODEL, D_MODEL)),
            full((1, D_MODEL)), full((D_MODEL, qw)), full(keys.shape),
        ],
        out_specs=[
            pl.BlockSpec((tm, D_MODEL), lambda i: (i, 0)),
            pl.BlockSpec((D_MODEL, tm), lambda i: (0, i)),
            pl.BlockSpec((2, PEER_N_KEYS, nlist // 2, tm), lambda i: (0, 0, 0, i)),
        ],
        out_shape=[
            jax.ShapeDtypeStruct((t, D_MODEL), F32),
            jax.ShapeDtypeStruct((D_MODEL, t), BF16),
            jax.ShapeDtypeStruct((2, PEER_N_KEYS, nlist // 2, t), F32),
        ],
        compiler_params=_cparams(("arbitrary",), 56),
        name="merge_query",
    )(ya, yb, gates, x2, wpa, wpb, wout, g_ffn, wq, keys)


def _staircase_cells():
    k = PEER_TOPK
    return [(r1, r2) for r1 in range(k) for r2 in range(k // (r1 + 1))]


def _first_max(items):
    level = list(items)
    while len(level) > 1:
        nxt = []
        for i in range(0, len(level) - 1, 2):
            (v0, i0), (v1, i1) = level[i], level[i + 1]
            later = v1 > v0
            nxt.append((jnp.where(later, v1, v0), jnp.where(later, i1, i0)))
        if len(level) % 2:
            nxt.append(level[-1])
        level = nxt
    return level[0]


def _select_kernel(s_ref, cnt_ref, ea_ref, rank2_ref, eb_ref, cur_ref, cand_ref, srt_ref,
                   first_ref, pick_ref):
    k = PEER_TOPK
    nk = s_ref.shape[1]
    tile = s_ref.shape[2:]
    none = jnp.full(tile, -1, jnp.int32)
    cells = _staircase_cells()

    cur_ref[...] = s_ref[...]

    def extract(it, prev):
        winners = []
        for l in range(2):
            items = []
            for key in range(nk):
                c = jnp.where(prev[l] == key, -jnp.inf, cur_ref[l, key])
                cur_ref[l, key] = c
                items.append((c, key))
            m, first = _first_max(items)
            srt_ref[l, pl.ds(it, 1)] = m[None]
            first_ref[l, pl.ds(it, 1)] = first[None]
            winners.append(first)
        return tuple(winners)

    lax.fori_loop(0, k, extract, (none, none))
    as_ = [srt_ref[0, r] for r in range(k)]
    bs_ = [srt_ref[1, r] for r in range(k)]

    for ci, (r1, r2) in enumerate(cells):
        cand_ref[ci] = as_[r1] + bs_[r2]
    top = as_[0] + bs_[0]

    def pick(it, carry):
        prev, z = carry
        items = []
        for ci, (r1, r2) in enumerate(cells):
            pos = r1 * k + r2
            c = jnp.where(prev == pos, -jnp.inf, cand_ref[ci])
            cand_ref[ci] = c
            items.append((c, pos))
        m, first = _first_max(items)
        pick_ref[pl.ds(it, 1)] = first[None]
        return first, z + jnp.exp(m - top)

    _, z = lax.fori_loop(0, k, pick, (none, jnp.zeros(tile, F32)))

    picked_row = [pick_ref[it] // k for it in range(k)]
    counts = []
    for r1 in range(k):
        c = jnp.zeros(tile, F32)
        for it in range(k):
            c = c + jnp.where(picked_row[it] == r1, 1.0, 0.0)
        counts.append(c)

    first_a = [first_ref[0, r] for r in range(k)]
    first_b = [first_ref[1, r] for r in range(k)]
    inv_z = 1.0 / z
    for key in range(nk):
        cnt = jnp.zeros(tile, F32)
        rank = jnp.full(tile, float(nk), F32)
        for r in range(k):
            cnt = jnp.where(first_a[r] == key, counts[r], cnt)
            rank = jnp.where(first_b[r] == key, float(r), rank)
        cnt_ref[key] = cnt
        rank2_ref[:, key, :] = rank
        ea_ref[key] = jnp.exp(s_ref[0, key] - as_[0])
        eb_ref[:, key, :] = jnp.exp(s_ref[1, key] - bs_[0]) * inv_z


def _select(scores):
    _, nk, heads, t = scores.shape
    k = PEER_TOPK
    tab = pl.BlockSpec((nk, heads, LANES), lambda i: (0, 0, i))
    shape = jax.ShapeDtypeStruct((nk, heads, t), F32)
    htab = pl.BlockSpec((heads, nk, LANES), lambda i: (0, 0, i))
    hshape = jax.ShapeDtypeStruct((heads, nk, t), F32)
    return pl.pallas_call(
        _select_kernel,
        grid=(t // LANES,),
        in_specs=[pl.BlockSpec((2, nk, heads, LANES), lambda i: (0, 0, 0, i))],
        out_specs=[tab, tab, htab, htab],
        out_shape=[shape, shape, hshape, hshape],
        scratch_shapes=[
            pltpu.VMEM((2, nk, heads, LANES), F32),
            pltpu.VMEM((len(_staircase_cells()), heads, LANES), F32),
            pltpu.VMEM((2, k, heads, LANES), F32),
            pltpu.VMEM((2, k, heads, LANES), jnp.int32),
            pltpu.VMEM((k, heads, LANES), jnp.int32),
        ],
        compiler_params=_cparams(("arbitrary",), 32),
        name="peer_select",
    )(scores)


_ERFC_P = 0.3275911
_ERFC_HALF_A = tuple(0.5 * a for a in (0.254829592, -0.284496736, 1.421413741, -1.453152027,
                                       1.061405429))


def _gelu(x):
    z = x * math.sqrt(0.5)
    t = 1.0 / (1.0 + _ERFC_P * jnp.abs(z))
    poly = _ERFC_HALF_A[4]
    for a in _ERFC_HALF_A[3::-1]:
        poly = poly * t + a
    xh = x * (poly * t * jnp.exp(-(z * z)))
    return jnp.where(x >= 0, x - xh, xh)


def _peer_gate_chunk(tc, j_rows, cnt_ref, ea_ref, tab_ref, at_ref, p_ref):
    cols = slice(tc * LANES, (tc + 1) * LANES)
    n_k = PEER_N_KEYS // BF16_SUBLANE_ROWS
    acc = [[None] * n_k for _ in j_rows]
    for h in range(PEER_HEADS):
        bcast = lambda ref, j: jnp.broadcast_to(
            ref[j, h:h + 1, cols], (BF16_SUBLANE_ROWS, LANES)).astype(BF16)
        cnt_b = [bcast(cnt_ref, j) for j in j_rows]
        ea_b = [bcast(ea_ref, j) for j in j_rows]
        for k in range(n_k):
            r0 = k * BF16_SUBLANE_ROWS
            r2 = tab_ref[h, tc, r0:r0 + BF16_SUBLANE_ROWS, :]
            ebh = tab_ref[h, tc, TAB_EB_ROW + r0:TAB_EB_ROW + r0 + BF16_SUBLANE_ROWS, :]
            for jj in range(len(j_rows)):
                term = jnp.where(r2 < cnt_b[jj], ebh, 0.0) * ea_b[jj]
                acc[jj][k] = term if acc[jj][k] is None else acc[jj][k] + term
    for jj, j in enumerate(j_rows):
        for k in range(n_k):
            rows = slice(j * PEER_N_KEYS + k * BF16_SUBLANE_ROWS,
                         j * PEER_N_KEYS + (k + 1) * BF16_SUBLANE_ROWS)
            act = at_ref[rows, cols]
            gelu = 0.5 * act * (1.0 + lax.erf(act * math.sqrt(0.5)))
            p_ref[rows, cols] = acc[jj][k] * gelu.astype(BF16)


def _peer_mix_kernel(final_norm, xnt_ref, u_ref, vt_ref, cnt_ref, ea_ref, rank2_ref, eb_ref,
                     h_ref, g_ref, o_ref, at_ref, p_ref, acc_ref, tab_ref):
    e = pl.program_id(1)
    eb, tt = p_ref.shape

    @pl.when(e == 0)
    def _():
        acc_ref[...] = jnp.zeros_like(acc_ref)
        for h in range(PEER_HEADS):
            for tc in range(tt // LANES):
                cols = slice(tc * LANES, (tc + 1) * LANES)
                tab_ref[h, tc, 0:PEER_N_KEYS, :] = rank2_ref[h, :, cols].astype(BF16)
                tab_ref[h, tc, TAB_EB_ROW:TAB_EB_ROW + PEER_N_KEYS, :] = eb_ref[h, :, cols].astype(BF16)

    rows_per_sub = SB_PEER // PEER_N_KEYS
    for sb in range(eb // SB_PEER):
        sub = slice(sb * SB_PEER, (sb + 1) * SB_PEER)
        at_ref[sub, 0:tt] = jnp.dot(u_ref[sub, :], xnt_ref[...], preferred_element_type=F32)
        for jg in range(sb * rows_per_sub, (sb + 1) * rows_per_sub, GATE_ROWS):
            for tc in range(tt // LANES):
                _peer_gate_chunk(tc, tuple(range(jg, jg + GATE_ROWS)), cnt_ref, ea_ref, tab_ref,
                                 at_ref, p_ref)
    acc_ref[...] += jnp.dot(vt_ref[...], p_ref[...], preferred_element_type=F32)

    @pl.when(e == pl.num_programs(1) - 1)
    def _():
        y = h_ref[...] + acc_ref[...].T
        if final_norm:
            ms = jnp.mean(y * y, axis=-1, keepdims=True)
            y = y * lax.rsqrt(ms + EPS) * g_ref[...]
        o_ref[...] = y


def _peer_mix(xnt, u_bf, vt_bf, cnt, ea, rank2, eb, h, g_final, final_norm):
    t = xnt.shape[1]
    tt, ebk = TT_PEER, EB_PEER
    n_exp = u_bf.shape[0]
    nk, heads, _ = cnt.shape
    tab = pl.BlockSpec((nk, heads, tt), lambda ti, e: (0, 0, ti))
    row_tab = pl.BlockSpec((ebk // nk, heads, tt), lambda ti, e: (e, 0, ti))
    htab = pl.BlockSpec((heads, nk, tt), lambda ti, e: (0, 0, ti))
    return pl.pallas_call(
        functools.partial(_peer_mix_kernel, final_norm),
        grid=(t // tt, n_exp // ebk),
        in_specs=[
            pl.BlockSpec((D_MODEL, tt), lambda ti, e: (0, ti)),
            pl.BlockSpec((ebk, D_MODEL), lambda ti, e: (e, 0)),
            pl.BlockSpec((None, D_MODEL, ebk), lambda ti, e: (e, 0, 0)),
            row_tab, row_tab, htab, htab,
            pl.BlockSpec((tt, D_MODEL), lambda ti, e: (ti, 0)),
            pl.BlockSpec((1, D_MODEL), lambda ti, e: (0, 0)),
        ],
        out_specs=pl.BlockSpec((tt, D_MODEL), lambda ti, e: (ti, 0)),
        out_shape=jax.ShapeDtypeStruct((t, D_MODEL), F32),
        scratch_shapes=[
            pltpu.VMEM((ebk, tt + LANES), F32),
            pltpu.VMEM((ebk, tt), BF16),
            pltpu.VMEM((D_MODEL, tt), F32),
            pltpu.VMEM((heads, tt // LANES, TAB_ROWS, LANES), BF16),
        ],
        compiler_params=_cparams(("arbitrary", "arbitrary"), 56),
        name="peer_mix",
    )(xnt, u_bf, vt_bf, cnt, ea, rank2, eb, h, g_final)


def _rope_tables(seq):
    half = ROPE_DIM // 2
    inv_freq = ROPE_THETA ** (-jnp.arange(0, ROPE_DIM, 2, dtype=F32) / ROPE_DIM)
    ang = jnp.arange(seq, dtype=F32)[:, None] * inv_freq[None, :]
    cos, sin = jnp.cos(ang), jnp.sin(ang)
    pad = jnp.zeros((seq, DIFF_HEAD_DIM - ROPE_DIM), F32)
    c64 = jnp.concatenate([cos, cos, pad + 1.0], axis=1)
    up64 = jnp.concatenate([jnp.zeros_like(sin), sin, pad], axis=1)
    dn64 = jnp.concatenate([-sin, jnp.zeros_like(sin), pad], axis=1)
    rep = LANES // DIFF_HEAD_DIM
    return jnp.tile(c64, (1, rep)), jnp.tile(up64, (1, rep)), jnp.tile(dn64, (1, rep))


def kernel(x, w_in, w_proj_a, w_proj_b, w_out, norm_mix, norm_ffn, norm_final, lambda_q1,
           lambda_k1, lambda_q2, lambda_k2, diff_subln, na_rpb, peer_w_query, peer_sub_keys,
           peer_u, peer_v):
    b, s, d = x.shape
    t = b * s
    depth = w_in.shape[0]
    rows = s // GRID_W
    kr = min(NA_KR_MAX, rows)
    cos_t, sin_up_t, sin_dn_t = _rope_tables(s)
    h = x.reshape(t, d)
    row = lambda v: v.reshape(1, -1).astype(F32)
    for l in range(depth):
        lam_init = 0.8 - 0.6 * math.exp(-0.3 * l)
        qkv, gates = _inproj(h, row(norm_mix[l]), w_in[l].astype(BF16), cos_t, sin_up_t,
                             sin_dn_t, s)
        qkv3 = qkv.reshape(b, s, QKV_W)
        ya = _diff_attn(qkv3, row(lambda_q1[l]), row(lambda_k1[l]), row(lambda_q2[l]),
                        row(lambda_k2[l]), row(diff_subln[l]), lam_init)
        yb = _na_attn(qkv3, _na_bias_table(na_rpb[l], rows, kr), rows, kr)
        keys = peer_sub_keys[l].reshape(2 * PEER_HEADS, PEER_N_KEYS, PEER_QUERY_DIM // 2)
        h_mid, xnt, scores_t = _merge(
            ya.reshape(t, A_V_W), yb.reshape(t, B_W), gates, h,
            w_proj_a[l].astype(BF16), w_proj_b[l].astype(BF16), w_out[l].astype(BF16),
            row(norm_ffn[l]), peer_w_query[l].astype(BF16), keys.astype(BF16))
        cnt, ea, rank2, eb = _select(scores_t)
        vt_blocks = peer_v[l].reshape(-1, EB_PEER, d).transpose(0, 2, 1).astype(BF16)
        h = _peer_mix(xnt, peer_u[l].astype(BF16), vt_blocks, cnt, ea, rank2, eb,
                      h_mid, row(norm_final), final_norm=(l == depth - 1))
    return h.reshape(b, s, d)
```

```python
import functools
import math

import jax
import jax.numpy as jnp
import numpy as np
from jax import lax
from jax.experimental import pallas as pl
from jax.experimental.pallas import tpu as pltpu

F32 = jnp.float32
BF16 = jnp.bfloat16

D_MODEL = 1024
GRID_W = 64
N_DIFF_HEADS = 4
DIFF_HEAD_DIM = 64
DIFF_V_DIM = 2 * DIFF_HEAD_DIM
ROPE_THETA = 500000.0
ROPE_DIM = DIFF_HEAD_DIM // 4
N_NA_HEADS = 8
NA_HEAD_DIM = 64
NA_KR_MAX = 8
NA_KC = 16
A_QK_W = N_DIFF_HEADS * 2 * DIFF_HEAD_DIM
A_V_W = N_DIFF_HEADS * DIFF_V_DIM
B_W = N_NA_HEADS * NA_HEAD_DIM
QKV_W = 2 * A_QK_W + A_V_W + 3 * B_W
GATE_W = 2 * D_MODEL
PEER_HEADS = 8
PEER_N_KEYS = 128
PEER_N_EXPERTS = PEER_N_KEYS * PEER_N_KEYS
PEER_QUERY_DIM = 256
PEER_TOPK = 16
EPS = 1e-6

LANES = 128
NEG_BIG = -1e30
MIB = 1024 * 1024

TM_INPROJ = 512
TQ_DIFF = 256
TM_MERGE = 256
TT_PEER = 512
EB_PEER = 2048
SB_PEER = 512
GATE_ROWS = 2
GATE_DTYPE = F32
GATE_VREG_ROWS = 8
BF16_SUBLANE_ROWS = 16
TAB_EB_ROW = PEER_N_KEYS + GATE_VREG_ROWS
TAB_ROWS = TAB_EB_ROW + PEER_N_KEYS


def _cparams(sem, vmem_mib, flags=None):
    return pltpu.CompilerParams(dimension_semantics=sem, vmem_limit_bytes=vmem_mib * MIB,
                                flags=flags)


def _inproj_kernel(x_ref, g_ref, w_ref, c_ref, sa_ref, sb_ref, qkv_ref, gate_ref):
    x = x_ref[...]
    ms = jnp.mean(x * x, axis=-1, keepdims=True)
    n = (x * lax.rsqrt(ms + EPS) * g_ref[...]).astype(BF16)
    chunk = 512
    reps = chunk // LANES
    cos = jnp.tile(c_ref[...], (1, reps))
    sin_up = jnp.tile(sa_ref[...], (1, reps))
    sin_dn = jnp.tile(sb_ref[...], (1, reps))
    half = ROPE_DIM // 2
    for j in range(QKV_W // chunk):
        y = jnp.dot(n, w_ref[:, j * chunk:(j + 1) * chunk], preferred_element_type=F32)
        if j < 2 * A_QK_W // chunk:
            y = (y * cos + pltpu.roll(y, half, axis=1) * sin_up
                 + pltpu.roll(y, chunk - half, axis=1) * sin_dn)
            if j < A_QK_W // chunk:
                y = y * (DIFF_HEAD_DIM ** -0.5)
        qkv_ref[:, j * chunk:(j + 1) * chunk] = y.astype(BF16)
    for j in range(GATE_W // chunk):
        c0 = QKV_W + j * chunk
        gate_ref[:, j * chunk:(j + 1) * chunk] = jnp.dot(
            n, w_ref[:, c0:c0 + chunk], preferred_element_type=F32)


def _inproj(x2, g, w_bf, cos_t, sin_up_t, sin_dn_t, seq):
    t = x2.shape[0]
    tm = TM_INPROJ
    nblk = seq // tm
    in_w = w_bf.shape[1]
    tab_spec = pl.BlockSpec((tm, LANES), lambda i: (i % nblk, 0))
    return pl.pallas_call(
        _inproj_kernel,
        grid=(t // tm,),
        in_specs=[
            pl.BlockSpec((tm, D_MODEL), lambda i: (i, 0)),
            pl.BlockSpec((1, D_MODEL), lambda i: (0, 0)),
            pl.BlockSpec((D_MODEL, in_w), lambda i: (0, 0)),
            tab_spec, tab_spec, tab_spec,
        ],
        out_specs=[
            pl.BlockSpec((tm, QKV_W), lambda i: (i, 0)),
            pl.BlockSpec((tm, GATE_W), lambda i: (i, 0)),
        ],
        out_shape=[
            jax.ShapeDtypeStruct((t, QKV_W), BF16),
            jax.ShapeDtypeStruct((t, GATE_W), F32),
        ],
        compiler_params=_cparams(("arbitrary",), 56),
        name="inproj",
    )(x2, g, w_bf, cos_t, sin_up_t, sin_dn_t)


def _diff_attn_kernel(lam_init, q_ref, k_ref, v_ref, lq1_ref, lk1_ref, lq2_ref, lk2_ref,
                      g_ref, o_ref):
    q = q_ref[...]
    k = k_ref[...]
    lane = lax.broadcasted_iota(jnp.int32, q.shape, 1)
    zero = jnp.zeros_like(q)
    q1 = jnp.where(lane < DIFF_HEAD_DIM, q, zero)
    q2 = jnp.where(lane >= DIFF_HEAD_DIM, q, zero)
    nt = (((1,), (1,)), ((), ()))
    s1 = lax.dot_general(q1, k, nt, preferred_element_type=F32)
    s2 = lax.dot_general(q2, k, nt, preferred_element_type=F32)
    e1 = jnp.exp(s1 - jnp.max(s1, axis=-1, keepdims=True))
    e2 = jnp.exp(s2 - jnp.max(s2, axis=-1, keepdims=True))
    l1 = jnp.sum(e1, axis=-1, keepdims=True)
    l2 = jnp.sum(e2, axis=-1, keepdims=True)
    lam = (jnp.exp(jnp.sum(lq1_ref[...] * lk1_ref[...], axis=-1, keepdims=True))
           - jnp.exp(jnp.sum(lq2_ref[...] * lk2_ref[...], axis=-1, keepdims=True))
           + lam_init)
    v = v_ref[...]
    o = (jnp.dot(e1.astype(BF16), v, preferred_element_type=F32) * (1.0 / l1)
         - jnp.dot(e2.astype(BF16), v, preferred_element_type=F32) * (lam / l2))
    ms = jnp.mean(o * o, axis=-1, keepdims=True)
    o = o * lax.rsqrt(ms + EPS) * g_ref[...] * (1.0 - lam_init)
    o_ref[...] = o.astype(o_ref.dtype)


def _diff_attn(qkv3, lq1, lk1, lq2, lk2, subln_g, lam_init):
    b, s, _ = qkv3.shape
    tq = TQ_DIFF
    hw = 2 * DIFF_HEAD_DIM
    k_blk0 = A_QK_W // hw
    v_blk0 = 2 * A_QK_W // hw
    vec = pl.BlockSpec((1, DIFF_HEAD_DIM), lambda bi, h, i: (0, 0))
    return pl.pallas_call(
        functools.partial(_diff_attn_kernel, lam_init),
        grid=(b, N_DIFF_HEADS, s // tq),
        in_specs=[
            pl.BlockSpec((None, tq, hw), lambda bi, h, i: (bi, i, h)),
            pl.BlockSpec((None, s, hw), lambda bi, h, i: (bi, 0, k_blk0 + h)),
            pl.BlockSpec((None, s, DIFF_V_DIM), lambda bi, h, i: (bi, 0, v_blk0 + h)),
            vec, vec, vec, vec,
            pl.BlockSpec((1, DIFF_V_DIM), lambda bi, h, i: (0, 0)),
        ],
        out_specs=pl.BlockSpec((None, tq, DIFF_V_DIM), lambda bi, h, i: (bi, i, h)),
        out_shape=jax.ShapeDtypeStruct((b, s, A_V_W), BF16),
        compiler_params=_cparams(("arbitrary", "arbitrary", "arbitrary"), 48),
        name="diff_attn",
    )(qkv3, qkv3, qkv3, lq1, lk1, lq2, lk2, subln_g)


def _na_window_start(r, rows, kr):
    return jnp.clip(r - kr // 2, 0, rows - kr)


def _na_kernel(rows, kr, q_ref, k_ref, v_ref, bias_ref, o_ref):
    r = pl.program_id(1)
    start = pl.multiple_of(_na_window_start(r, rows, kr) * GRID_W, GRID_W)
    nkeys = kr * GRID_W
    kwin = k_ref[pl.ds(start, nkeys), :]
    vwin = v_ref[pl.ds(start, nkeys), :]
    q = q_ref[...] * (NA_HEAD_DIM ** -0.5)
    nt = (((1,), (1,)), ((), ()))
    pair_w = 2 * NA_HEAD_DIM
    lane = lax.broadcasted_iota(jnp.int32, (GRID_W, pair_w), 1)
    scores = []
    for h in range(N_NA_HEADS):
        sl = slice((h // 2) * pair_w, (h // 2 + 1) * pair_w)
        keep = (lane < NA_HEAD_DIM) if h % 2 == 0 else (lane >= NA_HEAD_DIM)
        qm = jnp.where(keep, q[:, sl], jnp.zeros_like(q[:, sl]))
        s = lax.dot_general(qm, kwin[:, sl], nt, preferred_element_type=F32)
        scores.append(s + bias_ref[h])
    s = jnp.concatenate(scores, axis=0)
    e = jnp.exp(s - jnp.max(s, axis=-1, keepdims=True))
    p = (e * (1.0 / jnp.sum(e, axis=-1, keepdims=True))).astype(BF16)
    for hp in range(N_NA_HEADS // 2):
        sl = slice(hp * pair_w, (hp + 1) * pair_w)
        outs = [jnp.dot(p[(2 * hp + hh) * GRID_W:(2 * hp + hh + 1) * GRID_W], vwin[:, sl],
                        preferred_element_type=F32) for hh in range(2)]
        o_ref[:, sl] = jnp.where(lane < NA_HEAD_DIM, outs[0], outs[1]).astype(o_ref.dtype)


def _na_attn(qkv3, bias_tab, rows, kr):
    b, s, _ = qkv3.shape
    q_blk = (2 * A_QK_W + A_V_W) // B_W
    nkeys = kr * GRID_W

    def bias_idx(bi, r):
        return (0, r - _na_window_start(r, rows, kr), 0, 0)

    return pl.pallas_call(
        functools.partial(_na_kernel, rows, kr),
        grid=(b, rows),
        in_specs=[
            pl.BlockSpec((None, GRID_W, B_W), lambda bi, r: (bi, r, q_blk)),
            pl.BlockSpec((None, s, B_W), lambda bi, r: (bi, 0, q_blk + 1)),
            pl.BlockSpec((None, s, B_W), lambda bi, r: (bi, 0, q_blk + 2)),
            pl.BlockSpec((N_NA_HEADS, None, GRID_W, nkeys), bias_idx),
        ],
        out_specs=pl.BlockSpec((None, GRID_W, B_W), lambda bi, r: (bi, r, 0)),
        out_shape=jax.ShapeDtypeStruct((b, s, B_W), BF16),
        compiler_params=_cparams(("arbitrary", "arbitrary"), 40),
        name="na_attn",
    )(qkv3, qkv3, qkv3, bias_tab)


def _na_bias_table(rpb, rows, kr):
    w = np.arange(GRID_W)[:, None]
    c = np.arange(GRID_W)[None, :]
    cs = np.clip(w - NA_KC // 2, 0, GRID_W - NA_KC)
    valid = (c >= cs) & (c < cs + NA_KC)
    ncol = 2 * NA_KC - 1
    onehot = ((c - w + (NA_KC - 1))[None] == np.arange(ncol)[:, None, None]) & valid[None]
    top = NA_KR_MAX - 1
    rsel = jnp.stack([rpb[:, top - d:top - d + kr, :] for d in range(kr)], axis=1).astype(F32)
    tab = jnp.einsum('hdik,kwc->hdwic', rsel, jnp.asarray(onehot, F32),
                     precision=lax.Precision.HIGHEST)
    tab = jnp.where(valid[None, None, :, None, :], tab, NEG_BIG)
    return tab.reshape(rpb.shape[0], kr, GRID_W, kr * GRID_W)


def _merge_kernel(ya_ref, yb_ref, gate_ref, x_ref, wpa_ref, wpb_ref, wout_ref, g_ref, wq_ref,
                  keys_ref, h_ref, xnt_ref, st_ref):
    ma = jnp.dot(ya_ref[...], wpa_ref[...], preferred_element_type=F32)
    mb = jnp.dot(yb_ref[...], wpb_ref[...], preferred_element_type=F32)
    ga = gate_ref[:, :D_MODEL]
    gb = gate_ref[:, D_MODEL:]
    merged = jax.nn.sigmoid(ga) * ma + jax.nn.sigmoid(gb) * mb
    h = x_ref[...] + jnp.dot(merged.astype(BF16), wout_ref[...], preferred_element_type=F32)
    h_ref[...] = h
    ms = jnp.mean(h * h, axis=-1, keepdims=True)
    xn_f32 = h * lax.rsqrt(ms + EPS) * g_ref[...]
    xn = xn_f32.astype(BF16)
    xnt_ref[...] = xn_f32.T.astype(BF16)
    q = jnp.dot(xn, wq_ref[...], preferred_element_type=F32).astype(BF16)
    nt = (((1,), (1,)), ((), ()))
    half = PEER_QUERY_DIM // 2
    for hp in range(2 * PEER_HEADS):
        st_ref[hp % 2, :, hp // 2, :] = lax.dot_general(
            keys_ref[hp], q[:, hp * half:(hp + 1) * half], nt, preferred_element_type=F32)


def _merge(ya, yb, gates, x2, wpa, wpb, wout, g_ffn, wq, keys):
    t = x2.shape[0]
    tm = TM_MERGE
    qw = wq.shape[1]
    nlist = keys.shape[0]
    full = lambda shape: pl.BlockSpec(shape, lambda i: (0,) * len(shape))
    return pl.pallas_call(
        _merge_kernel,
        grid=(t // tm,),
        in_specs=[
            pl.BlockSpec((tm, A_V_W), lambda i: (i, 0)),
            pl.BlockSpec((tm, B_W), lambda i: (i, 0)),
            pl.BlockSpec((tm, GATE_W), lambda i: (i, 0)),
            pl.BlockSpec((tm, D_MODEL), lambda i: (i, 0)),
            full((A_V_W, D_MODEL)), full((B_W, D_MODEL)), full((D_MODEL, D_MODEL)),
            full((1, D_MODEL)), full((D_MODEL, qw)), full(keys.shape),
        ],
        out_specs=[
            pl.BlockSpec((tm, D_MODEL), lambda i: (i, 0)),
            pl.BlockSpec((D_MODEL, tm), lambda i: (0, i)),
            pl.BlockSpec((2, PEER_N_KEYS, nlist // 2, tm), lambda i: (0, 0, 0, i)),
        ],
        out_shape=[
            jax.ShapeDtypeStruct((t, D_MODEL), F32),
            jax.ShapeDtypeStruct((D_MODEL, t), BF16),
            jax.ShapeDtypeStruct((2, PEER_N_KEYS, nlist // 2, t), F32),
        ],
        compiler_params=_cparams(("arbitrary",), 56),
        name="merge_query",
    )(ya, yb, gates, x2, wpa, wpb, wout, g_ffn, wq, keys)


def _staircase_cells():
    k = PEER_TOPK
    return [(r1, r2) for r1 in range(k) for r2 in range(k // (r1 + 1))]


def _first_max(items):
    level = list(items)
    while len(level) > 1:
        nxt = []
        for i in range(0, len(level) - 1, 2):
            (v0, i0), (v1, i1) = level[i], level[i + 1]
            later = v1 > v0
            nxt.append((jnp.where(later, v1, v0), jnp.where(later, i1, i0)))
        if len(level) % 2:
            nxt.append(level[-1])
        level = nxt
    return level[0]


def _select_kernel(s_ref, cnt_ref, ea_ref, rank2_ref, eb_ref, cur_ref, cand_ref, srt_ref,
                   first_ref, pick_ref):
    k = PEER_TOPK
    nk = s_ref.shape[1]
    tile = s_ref.shape[2:]
    none = jnp.full(tile, -1, jnp.int32)
    cells = _staircase_cells()

    cur_ref[...] = s_ref[...]

    def extract(it, prev):
        winners = []
        for l in range(2):
            items = []
            for key in range(nk):
                c = jnp.where(prev[l] == key, -jnp.inf, cur_ref[l, key])
                cur_ref[l, key] = c
                items.append((c, key))
            m, first = _first_max(items)
            srt_ref[l, pl.ds(it, 1)] = m[None]
            first_ref[l, pl.ds(it, 1)] = first[None]
            winners.append(first)
        return tuple(winners)

    lax.fori_loop(0, k, extract, (none, none))
    as_ = [srt_ref[0, r] for r in range(k)]
    bs_ = [srt_ref[1, r] for r in range(k)]

    for ci, (r1, r2) in enumerate(cells):
        cand_ref[ci] = as_[r1] + bs_[r2]
    top = as_[0] + bs_[0]

    def pick(it, carry):
        prev, z = carry
        items = []
        for ci, (r1, r2) in enumerate(cells):
            pos = r1 * k + r2
            c = jnp.where(prev == pos, -jnp.inf, cand_ref[ci])
            cand_ref[ci] = c
            items.append((c, pos))
        m, first = _first_max(items)
        pick_ref[pl.ds(it, 1)] = first[None]
        return first, z + jnp.exp(m - top)

    _, z = lax.fori_loop(0, k, pick, (none, jnp.zeros(tile, F32)))

    picked_row = [pick_ref[it] // k for it in range(k)]
    counts = []
    for r1 in range(k):
        c = jnp.zeros(tile, F32)
        for it in range(k):
            c = c + jnp.where(picked_row[it] == r1, 1.0, 0.0)
        counts.append(c)

    first_a = [first_ref[0, r] for r in range(k)]
    first_b = [first_ref[1, r] for r in range(k)]
    inv_z = 1.0 / z
    for key in range(nk):
        cnt = jnp.zeros(tile, F32)
        rank = jnp.full(tile, float(nk), F32)
        for r in range(k):
            cnt = jnp.where(first_a[r] == key, counts[r], cnt)
            rank = jnp.where(first_b[r] == key, float(r), rank)
        cnt_ref[key] = cnt
        rank2_ref[:, key, :] = rank
        ea_ref[key] = jnp.exp(s_ref[0, key] - as_[0])
        eb_ref[:, key, :] = jnp.exp(s_ref[1, key] - bs_[0]) * inv_z


def _select(scores):
    _, nk, heads, t = scores.shape
    k = PEER_TOPK
    tab = pl.BlockSpec((nk, heads, LANES), lambda i: (0, 0, i))
    shape = jax.ShapeDtypeStruct((nk, heads, t), F32)
    htab = pl.BlockSpec((heads, nk, LANES), lambda i: (0, 0, i))
    hshape = jax.ShapeDtypeStruct((heads, nk, t), F32)
    return pl.pallas_call(
        _select_kernel,
        grid=(t // LANES,),
        in_specs=[pl.BlockSpec((2, nk, heads, LANES), lambda i: (0, 0, 0, i))],
        out_specs=[tab, tab, htab, htab],
        out_shape=[shape, shape, hshape, hshape],
        scratch_shapes=[
            pltpu.VMEM((2, nk, heads, LANES), F32),
            pltpu.VMEM((len(_staircase_cells()), heads, LANES), F32),
            pltpu.VMEM((2, k, heads, LANES), F32),
            pltpu.VMEM((2, k, heads, LANES), jnp.int32),
            pltpu.VMEM((k, heads, LANES), jnp.int32),
        ],
        compiler_params=_cparams(("arbitrary",), 32),
        name="peer_select",
    )(scores)


def _peer_gate_chunk(tc, j_rows, cnt_ref, ea_ref, tab_ref, at_ref, p_ref):
    cols = slice(tc * LANES, (tc + 1) * LANES)
    vr = GATE_VREG_ROWS
    n_k = PEER_N_KEYS // vr
    acc = [[None] * n_k for _ in j_rows]
    for h in range(PEER_HEADS):
        bcast = lambda ref, j: jnp.broadcast_to(ref[j, h:h + 1, cols], (vr, LANES)).astype(GATE_DTYPE)
        cnt_b = [bcast(cnt_ref, j) for j in j_rows]
        ea_b = [bcast(ea_ref, j) for j in j_rows]
        for k in range(n_k):
            r2 = tab_ref[h, tc, k * vr:(k + 1) * vr, :]
            ebh = tab_ref[h, tc, TAB_EB_ROW + k * vr:TAB_EB_ROW + (k + 1) * vr, :]
            for jj in range(len(j_rows)):
                term = jnp.where(r2 < cnt_b[jj], ebh, 0.0) * ea_b[jj]
                acc[jj][k] = term if acc[jj][k] is None else acc[jj][k] + term
    pack = BF16_SUBLANE_ROWS // vr
    for jj, j in enumerate(j_rows):
        for m in range(n_k // pack):
            r0 = j * PEER_N_KEYS + m * BF16_SUBLANE_ROWS
            rows = slice(r0, r0 + BF16_SUBLANE_ROWS)
            act = at_ref[rows, cols]
            gelu = 0.5 * act * (1.0 + lax.erf(act * math.sqrt(0.5)))
            w = jnp.concatenate(acc[jj][m * pack:(m + 1) * pack], axis=0)
            p_ref[rows, cols] = (w * gelu.astype(GATE_DTYPE)).astype(BF16)


def _peer_mix_kernel(final_norm, xnt_ref, u_ref, vt_ref, cnt_ref, ea_ref, rank2_ref, eb_ref,
                     h_ref, g_ref, o_ref, at_ref, p_ref, acc_ref, tab_ref):
    e = pl.program_id(1)
    eb, tt = p_ref.shape

    @pl.when(e == 0)
    def _():
        acc_ref[...] = jnp.zeros_like(acc_ref)
        for h in range(PEER_HEADS):
            for tc in range(tt // LANES):
                cols = slice(tc * LANES, (tc + 1) * LANES)
                tab_ref[h, tc, 0:PEER_N_KEYS, :] = rank2_ref[h, :, cols].astype(GATE_DTYPE)
                tab_ref[h, tc, TAB_EB_ROW:TAB_EB_ROW + PEER_N_KEYS, :] = eb_ref[h, :, cols].astype(GATE_DTYPE)

    rows_per_sub = SB_PEER // PEER_N_KEYS
    for sb in range(eb // SB_PEER):
        sub = slice(sb * SB_PEER, (sb + 1) * SB_PEER)
        at_ref[sub, 0:tt] = jnp.dot(u_ref[sub, :], xnt_ref[...], preferred_element_type=F32)
        for jg in range(sb * rows_per_sub, (sb + 1) * rows_per_sub, GATE_ROWS):
            for tc in range(tt // LANES):
                _peer_gate_chunk(tc, tuple(range(jg, jg + GATE_ROWS)), cnt_ref, ea_ref, tab_ref,
                                 at_ref, p_ref)
    acc_ref[...] += jnp.dot(vt_ref[...], p_ref[...], preferred_element_type=F32)

    @pl.when(e == pl.num_programs(1) - 1)
    def _():
        y = h_ref[...] + acc_ref[...].T
        if final_norm:
            ms = jnp.mean(y * y, axis=-1, keepdims=True)
            y = y * lax.rsqrt(ms + EPS) * g_ref[...]
        o_ref[...] = y


def _peer_mix(xnt, u_bf, vt_bf, cnt, ea, rank2, eb, h, g_final, final_norm):
    t = xnt.shape[1]
    tt, ebk = TT_PEER, EB_PEER
    n_exp = u_bf.shape[0]
    nk, heads, _ = cnt.shape
    tab = pl.BlockSpec((nk, heads, tt), lambda ti, e: (0, 0, ti))
    row_tab = pl.BlockSpec((ebk // nk, heads, tt), lambda ti, e: (e, 0, ti))
    htab = pl.BlockSpec((heads, nk, tt), lambda ti, e: (0, 0, ti))
    return pl.pallas_call(
        functools.partial(_peer_mix_kernel, final_norm),
        grid=(t // tt, n_exp // ebk),
        in_specs=[
            pl.BlockSpec((D_MODEL, tt), lambda ti, e: (0, ti)),
            pl.BlockSpec((ebk, D_MODEL), lambda ti, e: (e, 0)),
            pl.BlockSpec((None, D_MODEL, ebk), lambda ti, e: (e, 0, 0)),
            row_tab, row_tab, htab, htab,
            pl.BlockSpec((tt, D_MODEL), lambda ti, e: (ti, 0)),
            pl.BlockSpec((1, D_MODEL), lambda ti, e: (0, 0)),
        ],
        out_specs=pl.BlockSpec((tt, D_MODEL), lambda ti, e: (ti, 0)),
        out_shape=jax.ShapeDtypeStruct((t, D_MODEL), F32),
        scratch_shapes=[
            pltpu.VMEM((ebk, tt + LANES), F32),
            pltpu.VMEM((ebk, tt), BF16),
            pltpu.VMEM((D_MODEL, tt), F32),
            pltpu.VMEM((heads, tt // LANES, TAB_ROWS, LANES), GATE_DTYPE),
        ],
        compiler_params=_cparams(("arbitrary", "arbitrary"), 56),
        name="peer_mix",
    )(xnt, u_bf, vt_bf, cnt, ea, rank2, eb, h, g_final)


def _rope_tables(seq):
    half = ROPE_DIM // 2
    inv_freq = ROPE_THETA ** (-jnp.arange(0, ROPE_DIM, 2, dtype=F32) / ROPE_DIM)
    ang = jnp.arange(seq, dtype=F32)[:, None] * inv_freq[None, :]
    cos, sin = jnp.cos(ang), jnp.sin(ang)
    pad = jnp.zeros((seq, DIFF_HEAD_DIM - ROPE_DIM), F32)
    c64 = jnp.concatenate([cos, cos, pad + 1.0], axis=1)
    up64 = jnp.concatenate([jnp.zeros_like(sin), sin, pad], axis=1)
    dn64 = jnp.concatenate([-sin, jnp.zeros_like(sin), pad], axis=1)
    rep = LANES // DIFF_HEAD_DIM
    return jnp.tile(c64, (1, rep)), jnp.tile(up64, (1, rep)), jnp.tile(dn64, (1, rep))


def kernel(x, w_in, w_proj_a, w_proj_b, w_out, norm_mix, norm_ffn, norm_final, lambda_q1,
           lambda_k1, lambda_q2, lambda_k2, diff_subln, na_rpb, peer_w_query, peer_sub_keys,
           peer_u, peer_v):
    b, s, d = x.shape
    t = b * s
    depth = w_in.shape[0]
    rows = s // GRID_W
    kr = min(NA_KR_MAX, rows)
    cos_t, sin_up_t, sin_dn_t = _rope_tables(s)
    h = x.reshape(t, d)
    row = lambda v: v.reshape(1, -1).astype(F32)
    for l in range(depth):
        lam_init = 0.8 - 0.6 * math.exp(-0.3 * l)
        qkv, gates = _inproj(h, row(norm_mix[l]), w_in[l].astype(BF16), cos_t, sin_up_t,
                             sin_dn_t, s)
        qkv3 = qkv.reshape(b, s, QKV_W)
        ya = _diff_attn(qkv3, row(lambda_q1[l]), row(lambda_k1[l]), row(lambda_q2[l]),
                        row(lambda_k2[l]), row(diff_subln[l]), lam_init)
        yb = _na_attn(qkv3, _na_bias_table(na_rpb[l], rows, kr), rows, kr)
        keys = peer_sub_keys[l].reshape(2 * PEER_HEADS, PEER_N_KEYS, PEER_QUERY_DIM // 2)
        h_mid, xnt, scores_t = _merge(
            ya.reshape(t, A_V_W), yb.reshape(t, B_W), gates, h,
            w_proj_a[l].astype(BF16), w_proj_b[l].astype(BF16), w_out[l].astype(BF16),
            row(norm_ffn[l]), peer_w_query[l].astype(BF16), keys.astype(BF16))
        cnt, ea, rank2, eb = _select(scores_t)
        vt_blocks = peer_v[l].reshape(-1, EB_PEER, d).transpose(0, 2, 1).astype(BF16)
        h = _peer_mix(xnt, peer_u[l].astype(BF16), vt_blocks, cnt, ea, rank2, eb,
                      h_mid, row(norm_final), final_norm=(l == depth - 1))
    return h.reshape(b, s, d)
```

```python
import functools
import math

import jax
import jax.numpy as jnp
import numpy as np
from jax import lax
from jax.experimental import pallas as pl
from jax.experimental.pallas import tpu as pltpu

F32 = jnp.float32
BF16 = jnp.bfloat16

D_MODEL = 1024
GRID_W = 64
N_DIFF_HEADS = 4
DIFF_HEAD_DIM = 64
DIFF_V_DIM = 2 * DIFF_HEAD_DIM
ROPE_THETA = 500000.0
ROPE_DIM = DIFF_HEAD_DIM // 4
N_NA_HEADS = 8
NA_HEAD_DIM = 64
NA_KR_MAX = 8
NA_KC = 16
A_QK_W = N_DIFF_HEADS * 2 * DIFF_HEAD_DIM
A_V_W = N_DIFF_HEADS * DIFF_V_DIM
B_W = N_NA_HEADS * NA_HEAD_DIM
QKV_W = 2 * A_QK_W + A_V_W + 3 * B_W
GATE_W = 2 * D_MODEL
PEER_HEADS = 8
PEER_N_KEYS = 128
PEER_N_EXPERTS = PEER_N_KEYS * PEER_N_KEYS
PEER_QUERY_DIM = 256
PEER_TOPK = 16
EPS = 1e-6

LANES = 128
NEG_BIG = -1e30
MIB = 1024 * 1024

TM_INPROJ = 512
TQ_DIFF = 512
TM_MERGE = 512
TT_PEER = 512
EB_PEER = 2048
SB_PEER = 512
GATE_ROWS = 2
GATE_DTYPE = F32
GATE_VREG_ROWS = 8
BF16_SUBLANE_ROWS = 16
TAB_EB_ROW = PEER_N_KEYS + GATE_VREG_ROWS
TAB_ROWS = TAB_EB_ROW + PEER_N_KEYS


def _cparams(sem, vmem_mib):
    return pltpu.CompilerParams(dimension_semantics=sem, vmem_limit_bytes=vmem_mib * MIB)


def _inproj_kernel(x_ref, g_ref, w_ref, c_ref, sa_ref, sb_ref, qkv_ref, gate_ref):
    x = x_ref[...]
    ms = jnp.mean(x * x, axis=-1, keepdims=True)
    n = (x * lax.rsqrt(ms + EPS) * g_ref[...]).astype(BF16)
    chunk = 512
    reps = chunk // LANES
    cos = jnp.tile(c_ref[...], (1, reps))
    sin_up = jnp.tile(sa_ref[...], (1, reps))
    sin_dn = jnp.tile(sb_ref[...], (1, reps))
    half = ROPE_DIM // 2
    for j in range(QKV_W // chunk):
        y = jnp.dot(n, w_ref[:, j * chunk:(j + 1) * chunk], preferred_element_type=F32)
        if j < 2 * A_QK_W // chunk:
            y = (y * cos + pltpu.roll(y, half, axis=1) * sin_up
                 + pltpu.roll(y, chunk - half, axis=1) * sin_dn)
            if j < A_QK_W // chunk:
                y = y * (DIFF_HEAD_DIM ** -0.5)
        qkv_ref[:, j * chunk:(j + 1) * chunk] = y.astype(BF16)
    for j in range(GATE_W // chunk):
        c0 = QKV_W + j * chunk
        gate_ref[:, j * chunk:(j + 1) * chunk] = jnp.dot(
            n, w_ref[:, c0:c0 + chunk], preferred_element_type=F32)


def _inproj(x2, g, w_bf, cos_t, sin_up_t, sin_dn_t, seq):
    t = x2.shape[0]
    tm = TM_INPROJ
    nblk = seq // tm
    in_w = w_bf.shape[1]
    tab_spec = pl.BlockSpec((tm, LANES), lambda i: (i % nblk, 0))
    return pl.pallas_call(
        _inproj_kernel,
        grid=(t // tm,),
        in_specs=[
            pl.BlockSpec((tm, D_MODEL), lambda i: (i, 0)),
            pl.BlockSpec((1, D_MODEL), lambda i: (0, 0)),
            pl.BlockSpec((D_MODEL, in_w), lambda i: (0, 0)),
            tab_spec, tab_spec, tab_spec,
        ],
        out_specs=[
            pl.BlockSpec((tm, QKV_W), lambda i: (i, 0)),
            pl.BlockSpec((tm, GATE_W), lambda i: (i, 0)),
        ],
        out_shape=[
            jax.ShapeDtypeStruct((t, QKV_W), BF16),
            jax.ShapeDtypeStruct((t, GATE_W), F32),
        ],
        compiler_params=_cparams(("arbitrary",), 56),
        name="inproj",
    )(x2, g, w_bf, cos_t, sin_up_t, sin_dn_t)


def _diff_attn_kernel(lam_init, q_ref, k_ref, v_ref, lq1_ref, lk1_ref, lq2_ref, lk2_ref,
                      g_ref, o_ref):
    q = q_ref[...]
    k = k_ref[...]
    lane = lax.broadcasted_iota(jnp.int32, q.shape, 1)
    zero = jnp.zeros_like(q)
    q1 = jnp.where(lane < DIFF_HEAD_DIM, q, zero)
    q2 = jnp.where(lane >= DIFF_HEAD_DIM, q, zero)
    nt = (((1,), (1,)), ((), ()))
    s1 = lax.dot_general(q1, k, nt, preferred_element_type=F32)
    s2 = lax.dot_general(q2, k, nt, preferred_element_type=F32)
    e1 = jnp.exp(s1 - jnp.max(s1, axis=-1, keepdims=True))
    e2 = jnp.exp(s2 - jnp.max(s2, axis=-1, keepdims=True))
    l1 = jnp.sum(e1, axis=-1, keepdims=True)
    l2 = jnp.sum(e2, axis=-1, keepdims=True)
    lam = (jnp.exp(jnp.sum(lq1_ref[...] * lk1_ref[...], axis=-1, keepdims=True))
           - jnp.exp(jnp.sum(lq2_ref[...] * lk2_ref[...], axis=-1, keepdims=True))
           + lam_init)
    v = v_ref[...]
    o = (jnp.dot(e1.astype(BF16), v, preferred_element_type=F32) * (1.0 / l1)
         - jnp.dot(e2.astype(BF16), v, preferred_element_type=F32) * (lam / l2))
    ms = jnp.mean(o * o, axis=-1, keepdims=True)
    o = o * lax.rsqrt(ms + EPS) * g_ref[...] * (1.0 - lam_init)
    o_ref[...] = o.astype(o_ref.dtype)


def _diff_attn(qkv3, lq1, lk1, lq2, lk2, subln_g, lam_init):
    b, s, _ = qkv3.shape
    tq = TQ_DIFF
    hw = 2 * DIFF_HEAD_DIM
    k_blk0 = A_QK_W // hw
    v_blk0 = 2 * A_QK_W // hw
    vec = pl.BlockSpec((1, DIFF_HEAD_DIM), lambda bi, h, i: (0, 0))
    return pl.pallas_call(
        functools.partial(_diff_attn_kernel, lam_init),
        grid=(b, N_DIFF_HEADS, s // tq),
        in_specs=[
            pl.BlockSpec((None, tq, hw), lambda bi, h, i: (bi, i, h)),
            pl.BlockSpec((None, s, hw), lambda bi, h, i: (bi, 0, k_blk0 + h)),
            pl.BlockSpec((None, s, DIFF_V_DIM), lambda bi, h, i: (bi, 0, v_blk0 + h)),
            vec, vec, vec, vec,
            pl.BlockSpec((1, DIFF_V_DIM), lambda bi, h, i: (0, 0)),
        ],
        out_specs=pl.BlockSpec((None, tq, DIFF_V_DIM), lambda bi, h, i: (bi, i, h)),
        out_shape=jax.ShapeDtypeStruct((b, s, A_V_W), BF16),
        compiler_params=_cparams(("arbitrary", "arbitrary", "arbitrary"), 48),
        name="diff_attn",
    )(qkv3, qkv3, qkv3, lq1, lk1, lq2, lk2, subln_g)


def _na_window_start(r, rows, kr):
    return jnp.clip(r - kr // 2, 0, rows - kr)


def _na_kernel(rows, kr, q_ref, k_ref, v_ref, bias_ref, o_ref):
    r = pl.program_id(1)
    start = pl.multiple_of(_na_window_start(r, rows, kr) * GRID_W, GRID_W)
    nkeys = kr * GRID_W
    kwin = k_ref[pl.ds(start, nkeys), :]
    vwin = v_ref[pl.ds(start, nkeys), :]
    q = q_ref[...] * (NA_HEAD_DIM ** -0.5)
    nt = (((1,), (1,)), ((), ()))
    pair_w = 2 * NA_HEAD_DIM
    lane = lax.broadcasted_iota(jnp.int32, (GRID_W, pair_w), 1)
    scores = []
    for h in range(N_NA_HEADS):
        sl = slice((h // 2) * pair_w, (h // 2 + 1) * pair_w)
        keep = (lane < NA_HEAD_DIM) if h % 2 == 0 else (lane >= NA_HEAD_DIM)
        qm = jnp.where(keep, q[:, sl], jnp.zeros_like(q[:, sl]))
        s = lax.dot_general(qm, kwin[:, sl], nt, preferred_element_type=F32)
        scores.append(s + bias_ref[h])
    s = jnp.concatenate(scores, axis=0)
    e = jnp.exp(s - jnp.max(s, axis=-1, keepdims=True))
    p = (e * (1.0 / jnp.sum(e, axis=-1, keepdims=True))).astype(BF16)
    for hp in range(N_NA_HEADS // 2):
        sl = slice(hp * pair_w, (hp + 1) * pair_w)
        outs = [jnp.dot(p[(2 * hp + hh) * GRID_W:(2 * hp + hh + 1) * GRID_W], vwin[:, sl],
                        preferred_element_type=F32) for hh in range(2)]
        o_ref[:, sl] = jnp.where(lane < NA_HEAD_DIM, outs[0], outs[1]).astype(o_ref.dtype)


def _na_attn(qkv3, bias_tab, rows, kr):
    b, s, _ = qkv3.shape
    q_blk = (2 * A_QK_W + A_V_W) // B_W
    nkeys = kr * GRID_W

    def bias_idx(bi, r):
        return (0, r - _na_window_start(r, rows, kr), 0, 0)

    return pl.pallas_call(
        functools.partial(_na_kernel, rows, kr),
        grid=(b, rows),
        in_specs=[
            pl.BlockSpec((None, GRID_W, B_W), lambda bi, r: (bi, r, q_blk)),
            pl.BlockSpec((None, s, B_W), lambda bi, r: (bi, 0, q_blk + 1)),
            pl.BlockSpec((None, s, B_W), lambda bi, r: (bi, 0, q_blk + 2)),
            pl.BlockSpec((N_NA_HEADS, None, GRID_W, nkeys), bias_idx),
        ],
        out_specs=pl.BlockSpec((None, GRID_W, B_W), lambda bi, r: (bi, r, 0)),
        out_shape=jax.ShapeDtypeStruct((b, s, B_W), BF16),
        compiler_params=_cparams(("arbitrary", "arbitrary"), 40),
        name="na_attn",
    )(qkv3, qkv3, qkv3, bias_tab)


def _na_bias_table(rpb, rows, kr):
    w = np.arange(GRID_W)[:, None]
    c = np.arange(GRID_W)[None, :]
    cs = np.clip(w - NA_KC // 2, 0, GRID_W - NA_KC)
    valid = (c >= cs) & (c < cs + NA_KC)
    ncol = 2 * NA_KC - 1
    onehot = ((c - w + (NA_KC - 1))[None] == np.arange(ncol)[:, None, None]) & valid[None]
    top = NA_KR_MAX - 1
    rsel = jnp.stack([rpb[:, top - d:top - d + kr, :] for d in range(kr)], axis=1).astype(F32)
    tab = jnp.einsum('hdik,kwc->hdwic', rsel, jnp.asarray(onehot, F32),
                     precision=lax.Precision.HIGHEST)
    tab = jnp.where(valid[None, None, :, None, :], tab, NEG_BIG)
    return tab.reshape(rpb.shape[0], kr, GRID_W, kr * GRID_W)


def _merge_kernel(ya_ref, yb_ref, gate_ref, x_ref, wpa_ref, wpb_ref, wout_ref, g_ref, wq_ref,
                  keys_ref, h_ref, xnt_ref, st_ref):
    ma = jnp.dot(ya_ref[...], wpa_ref[...], preferred_element_type=F32)
    mb = jnp.dot(yb_ref[...], wpb_ref[...], preferred_element_type=F32)
    ga = gate_ref[:, :D_MODEL]
    gb = gate_ref[:, D_MODEL:]
    merged = jax.nn.sigmoid(ga) * ma + jax.nn.sigmoid(gb) * mb
    h = x_ref[...] + jnp.dot(merged.astype(BF16), wout_ref[...], preferred_element_type=F32)
    h_ref[...] = h
    ms = jnp.mean(h * h, axis=-1, keepdims=True)
    xn_f32 = h * lax.rsqrt(ms + EPS) * g_ref[...]
    xn = xn_f32.astype(BF16)
    xnt_ref[...] = xn_f32.T.astype(BF16)
    q = jnp.dot(xn, wq_ref[...], preferred_element_type=F32).astype(BF16)
    nt = (((1,), (1,)), ((), ()))
    half = PEER_QUERY_DIM // 2
    for hp in range(2 * PEER_HEADS):
        st_ref[hp % 2, :, hp // 2, :] = lax.dot_general(
            keys_ref[hp], q[:, hp * half:(hp + 1) * half], nt, preferred_element_type=F32)


def _merge(ya, yb, gates, x2, wpa, wpb, wout, g_ffn, wq, keys):
    t = x2.shape[0]
    tm = TM_MERGE
    qw = wq.shape[1]
    nlist = keys.shape[0]
    full = lambda shape: pl.BlockSpec(shape, lambda i: (0,) * len(shape))
    return pl.pallas_call(
        _merge_kernel,
        grid=(t // tm,),
        in_specs=[
            pl.BlockSpec((tm, A_V_W), lambda i: (i, 0)),
            pl.BlockSpec((tm, B_W), lambda i: (i, 0)),
            pl.BlockSpec((tm, GATE_W), lambda i: (i, 0)),
            pl.BlockSpec((tm, D_MODEL), lambda i: (i, 0)),
            full((A_V_W, D_MODEL)), full((B_W, D_MODEL)), full((D_MODEL, D_MODEL)),
            full((1, D_MODEL)), full((D_MODEL, qw)), full(keys.shape),
        ],
        out_specs=[
            pl.BlockSpec((tm, D_MODEL), lambda i: (i, 0)),
            pl.BlockSpec((D_MODEL, tm), lambda i: (0, i)),
            pl.BlockSpec((2, PEER_N_KEYS, nlist // 2, tm), lambda i: (0, 0, 0, i)),
        ],
        out_shape=[
            jax.ShapeDtypeStruct((t, D_MODEL), F32),
            jax.ShapeDtypeStruct((D_MODEL, t), BF16),
            jax.ShapeDtypeStruct((2, PEER_N_KEYS, nlist // 2, t), F32),
        ],
        compiler_params=_cparams(("arbitrary",), 56),
        name="merge_query",
    )(ya, yb, gates, x2, wpa, wpb, wout, g_ffn, wq, keys)


def _staircase_cells():
    k = PEER_TOPK
    return [(r1, r2) for r1 in range(k) for r2 in range(k // (r1 + 1))]


def _first_max(items):
    level = list(items)
    while len(level) > 1:
        nxt = []
        for i in range(0, len(level) - 1, 2):
            (v0, i0), (v1, i1) = level[i], level[i + 1]
            later = v1 > v0
            nxt.append((jnp.where(later, v1, v0), jnp.where(later, i1, i0)))
        if len(level) % 2:
            nxt.append(level[-1])
        level = nxt
    return level[0]


def _sort_desc(vals):
    vals = list(vals)
    n = len(vals)
    size = 2
    while size <= n:
        stride = size // 2
        while stride >= 1:
            for i in range(n):
                j = i ^ stride
                if j > i:
                    hi, lo = jnp.maximum(vals[i], vals[j]), jnp.minimum(vals[i], vals[j])
                    vals[i], vals[j] = (hi, lo) if (i & size) == 0 else (lo, hi)
            stride //= 2
        size *= 2
    return vals


def _merge_top(a, b):
    n = len(a)
    vals = [jnp.maximum(a[i], b[n - 1 - i]) for i in range(n)]
    stride = n // 2
    while stride >= 1:
        for i in range(n):
            j = i ^ stride
            if j > i:
                vals[i], vals[j] = jnp.maximum(vals[i], vals[j]), jnp.minimum(vals[i], vals[j])
        stride //= 2
    return vals


def _top_sorted(vals, k):
    groups = [_sort_desc(vals[i:i + k]) for i in range(0, len(vals), k)]
    while len(groups) > 1:
        groups = [_merge_top(groups[i], groups[i + 1]) for i in range(0, len(groups), 2)]
    return groups[0]


def _staircase(as_, bs_, cand_ref, pick_ref):
    k = PEER_TOPK
    tile = as_[0].shape
    cells = _staircase_cells()
    for ci, (r1, r2) in enumerate(cells):
        cand_ref[ci] = as_[r1] + bs_[r2]
    top = as_[0] + bs_[0]

    def pick(it, carry):
        prev, z = carry
        items = []
        for ci, (r1, r2) in enumerate(cells):
            pos = r1 * k + r2
            c = jnp.where(prev == pos, -jnp.inf, cand_ref[ci])
            cand_ref[ci] = c
            items.append((c, pos))
        m, first = _first_max(items)
        pick_ref[pl.ds(it, 1)] = first[None]
        return first, z + jnp.exp(m - top)

    _, z = lax.fori_loop(0, k, pick, (jnp.full(tile, -1, jnp.int32), jnp.zeros(tile, F32)))
    picked_row = [pick_ref[it] // k for it in range(k)]
    counts = []
    for r1 in range(k):
        c = jnp.zeros(tile, F32)
        for it in range(k):
            c = c + jnp.where(picked_row[it] == r1, 1.0, 0.0)
        counts.append(c)
    return counts, z


def _select_kernel(s_ref, cnt_ref, ea_ref, rank2_ref, eb_ref, cur_ref, cand_ref, srt_ref,
                   first_ref, pick_ref):
    k = PEER_TOPK
    nk = s_ref.shape[1]
    tile = s_ref.shape[2:]

    as_ = _top_sorted([s_ref[0, key] for key in range(nk)], k)
    bs_ = _top_sorted([s_ref[1, key] for key in range(nk)], k)
    counts, z = _staircase(as_, bs_, cand_ref, pick_ref)
    inv_z = 1.0 / z
    n_top = [jnp.zeros(tile, F32), jnp.zeros(tile, F32)]
    for key in range(nk):
        a, b = s_ref[0, key], s_ref[1, key]
        cnt = jnp.zeros(tile, F32)
        rank = jnp.full(tile, float(nk), F32)
        for r in range(k - 1, -1, -1):
            cnt = jnp.where(a >= as_[r], counts[r], cnt)
            rank = jnp.where(b >= bs_[r], float(r), rank)
        n_top[0] = n_top[0] + jnp.where(a >= as_[k - 1], 1.0, 0.0)
        n_top[1] = n_top[1] + jnp.where(b >= bs_[k - 1], 1.0, 0.0)
        cnt_ref[key] = cnt
        rank2_ref[:, key, :] = rank
        ea_ref[key] = jnp.exp(a - as_[0])
        eb_ref[:, key, :] = jnp.exp(b - bs_[0]) * inv_z

    tied = jnp.zeros(tile, F32)
    for srt, n in ((as_, n_top[0]), (bs_, n_top[1])):
        tied = jnp.where(n != float(k), 1.0, tied)
        for r in range(k - 1):
            tied = jnp.where(srt[r] == srt[r + 1], 1.0, tied)
    any_tied = jnp.max(tied)

    @pl.when(any_tied > 0.0)
    def _():
        _select_exact(s_ref, cnt_ref, ea_ref, rank2_ref, eb_ref, cur_ref, cand_ref, srt_ref,
                      first_ref, pick_ref)


def _select_exact(s_ref, cnt_ref, ea_ref, rank2_ref, eb_ref, cur_ref, cand_ref, srt_ref,
                  first_ref, pick_ref):
    k = PEER_TOPK
    nk = s_ref.shape[1]
    tile = s_ref.shape[2:]
    none = jnp.full(tile, -1, jnp.int32)

    cur_ref[...] = s_ref[...]

    def extract(it, prev):
        winners = []
        for l in range(2):
            items = []
            for key in range(nk):
                c = jnp.where(prev[l] == key, -jnp.inf, cur_ref[l, key])
                cur_ref[l, key] = c
                items.append((c, key))
            m, first = _first_max(items)
            srt_ref[l, pl.ds(it, 1)] = m[None]
            first_ref[l, pl.ds(it, 1)] = first[None]
            winners.append(first)
        return tuple(winners)

    lax.fori_loop(0, k, extract, (none, none))
    as_ = [srt_ref[0, r] for r in range(k)]
    bs_ = [srt_ref[1, r] for r in range(k)]
    counts, z = _staircase(as_, bs_, cand_ref, pick_ref)

    first_a = [first_ref[0, r] for r in range(k)]
    first_b = [first_ref[1, r] for r in range(k)]
    inv_z = 1.0 / z
    for key in range(nk):
        cnt = jnp.zeros(tile, F32)
        rank = jnp.full(tile, float(nk), F32)
        for r in range(k):
            cnt = jnp.where(first_a[r] == key, counts[r], cnt)
            rank = jnp.where(first_b[r] == key, float(r), rank)
        cnt_ref[key] = cnt
        rank2_ref[:, key, :] = rank
        ea_ref[key] = jnp.exp(s_ref[0, key] - as_[0])
        eb_ref[:, key, :] = jnp.exp(s_ref[1, key] - bs_[0]) * inv_z


def _select(scores):
    _, nk, heads, t = scores.shape
    k = PEER_TOPK
    tab = pl.BlockSpec((nk, heads, LANES), lambda i: (0, 0, i))
    shape = jax.ShapeDtypeStruct((nk, heads, t), F32)
    htab = pl.BlockSpec((heads, nk, LANES), lambda i: (0, 0, i))
    hshape = jax.ShapeDtypeStruct((heads, nk, t), F32)
    return pl.pallas_call(
        _select_kernel,
        grid=(t // LANES,),
        in_specs=[pl.BlockSpec((2, nk, heads, LANES), lambda i: (0, 0, 0, i))],
        out_specs=[tab, tab, htab, htab],
        out_shape=[shape, shape, hshape, hshape],
        scratch_shapes=[
            pltpu.VMEM((2, nk, heads, LANES), F32),
            pltpu.VMEM((len(_staircase_cells()), heads, LANES), F32),
            pltpu.VMEM((2, k, heads, LANES), F32),
            pltpu.VMEM((2, k, heads, LANES), jnp.int32),
            pltpu.VMEM((k, heads, LANES), jnp.int32),
        ],
        compiler_params=_cparams(("arbitrary",), 32),
        name="peer_select",
    )(scores)


def _peer_gate_chunk(tc, j_rows, cnt_ref, ea_ref, tab_ref, at_ref, p_ref):
    cols = slice(tc * LANES, (tc + 1) * LANES)
    vr = GATE_VREG_ROWS
    n_k = PEER_N_KEYS // vr
    acc = [[None] * n_k for _ in j_rows]
    for h in range(PEER_HEADS):
        bcast = lambda ref, j: jnp.broadcast_to(ref[j, h:h + 1, cols], (vr, LANES)).astype(GATE_DTYPE)
        cnt_b = [bcast(cnt_ref, j) for j in j_rows]
        ea_b = [bcast(ea_ref, j) for j in j_rows]
        for k in range(n_k):
            r2 = tab_ref[h, tc, k * vr:(k + 1) * vr, :]
            ebh = tab_ref[h, tc, TAB_EB_ROW + k * vr:TAB_EB_ROW + (k + 1) * vr, :]
            for jj in range(len(j_rows)):
                term = jnp.where(r2 < cnt_b[jj], ebh, 0.0) * ea_b[jj]
                acc[jj][k] = term if acc[jj][k] is None else acc[jj][k] + term
    pack = BF16_SUBLANE_ROWS // vr
    for jj, j in enumerate(j_rows):
        for m in range(n_k // pack):
            r0 = j * PEER_N_KEYS + m * BF16_SUBLANE_ROWS
            rows = slice(r0, r0 + BF16_SUBLANE_ROWS)
            act = at_ref[rows, cols]
            gelu = 0.5 * act * (1.0 + lax.erf(act * math.sqrt(0.5)))
            w = jnp.concatenate(acc[jj][m * pack:(m + 1) * pack], axis=0)
            p_ref[rows, cols] = (w * gelu.astype(GATE_DTYPE)).astype(BF16)


def _peer_mix_kernel(final_norm, xnt_ref, u_ref, vt_ref, cnt_ref, ea_ref, rank2_ref, eb_ref,
                     h_ref, g_ref, o_ref, at_ref, p_ref, acc_ref, tab_ref):
    e = pl.program_id(1)
    eb, tt = u_ref.shape[0], xnt_ref.shape[1]

    @pl.when(e == 0)
    def _():
        acc_ref[...] = jnp.zeros_like(acc_ref)
        for h in range(PEER_HEADS):
            for tc in range(tt // LANES):
                cols = slice(tc * LANES, (tc + 1) * LANES)
                tab_ref[h, tc, 0:PEER_N_KEYS, :] = rank2_ref[h, :, cols].astype(GATE_DTYPE)
                tab_ref[h, tc, TAB_EB_ROW:TAB_EB_ROW + PEER_N_KEYS, :] = eb_ref[h, :, cols].astype(GATE_DTYPE)

    rows_per_sub = SB_PEER // PEER_N_KEYS
    for sb in range(eb // SB_PEER):
        sub = slice(sb * SB_PEER, (sb + 1) * SB_PEER)
        at_ref[sub, 0:tt] = jnp.dot(u_ref[sub, :], xnt_ref[...], preferred_element_type=F32)
        for jg in range(sb * rows_per_sub, (sb + 1) * rows_per_sub, GATE_ROWS):
            for tc in range(tt // LANES):
                _peer_gate_chunk(tc, tuple(range(jg, jg + GATE_ROWS)), cnt_ref, ea_ref, tab_ref,
                                 at_ref, p_ref)
    acc_ref[...] += jnp.dot(vt_ref[...], p_ref[:, 0:tt], preferred_element_type=F32)

    @pl.when(e == pl.num_programs(1) - 1)
    def _():
        y = h_ref[...] + acc_ref[...].T
        if final_norm:
            ms = jnp.mean(y * y, axis=-1, keepdims=True)
            y = y * lax.rsqrt(ms + EPS) * g_ref[...]
        o_ref[...] = y


def _peer_mix(xnt, u_bf, vt_bf, cnt, ea, rank2, eb, h, g_final, final_norm):
    t = xnt.shape[1]
    tt, ebk = TT_PEER, EB_PEER
    n_exp = u_bf.shape[0]
    nk, heads, _ = cnt.shape
    tab = pl.BlockSpec((nk, heads, tt), lambda ti, e: (0, 0, ti))
    row_tab = pl.BlockSpec((ebk // nk, heads, tt), lambda ti, e: (e, 0, ti))
    htab = pl.BlockSpec((heads, nk, tt), lambda ti, e: (0, 0, ti))
    return pl.pallas_call(
        functools.partial(_peer_mix_kernel, final_norm),
        grid=(t // tt, n_exp // ebk),
        in_specs=[
            pl.BlockSpec((D_MODEL, tt), lambda ti, e: (0, ti)),
            pl.BlockSpec((ebk, D_MODEL), lambda ti, e: (e, 0)),
            pl.BlockSpec((None, D_MODEL, ebk), lambda ti, e: (e, 0, 0)),
            row_tab, row_tab, htab, htab,
            pl.BlockSpec((tt, D_MODEL), lambda ti, e: (ti, 0)),
            pl.BlockSpec((1, D_MODEL), lambda ti, e: (0, 0)),
        ],
        out_specs=pl.BlockSpec((tt, D_MODEL), lambda ti, e: (ti, 0)),
        out_shape=jax.ShapeDtypeStruct((t, D_MODEL), F32),
        scratch_shapes=[
            pltpu.VMEM((ebk, tt + LANES), F32),
            pltpu.VMEM((ebk, tt + LANES), BF16),
            pltpu.VMEM((D_MODEL, tt), F32),
            pltpu.VMEM((heads, tt // LANES, TAB_ROWS, LANES), GATE_DTYPE),
        ],
        compiler_params=_cparams(("arbitrary", "arbitrary"), 56),
        name="peer_mix",
    )(xnt, u_bf, vt_bf, cnt, ea, rank2, eb, h, g_final)


def _rope_tables(seq):
    half = ROPE_DIM // 2
    inv_freq = ROPE_THETA ** (-jnp.arange(0, ROPE_DIM, 2, dtype=F32) / ROPE_DIM)
    ang = jnp.arange(seq, dtype=F32)[:, None] * inv_freq[None, :]
    cos, sin = jnp.cos(ang), jnp.sin(ang)
    pad = jnp.zeros((seq, DIFF_HEAD_DIM - ROPE_DIM), F32)
    c64 = jnp.concatenate([cos, cos, pad + 1.0], axis=1)
    up64 = jnp.concatenate([jnp.zeros_like(sin), sin, pad], axis=1)
    dn64 = jnp.concatenate([-sin, jnp.zeros_like(sin), pad], axis=1)
    rep = LANES // DIFF_HEAD_DIM
    return jnp.tile(c64, (1, rep)), jnp.tile(up64, (1, rep)), jnp.tile(dn64, (1, rep))


def kernel(x, w_in, w_proj_a, w_proj_b, w_out, norm_mix, norm_ffn, norm_final, lambda_q1,
           lambda_k1, lambda_q2, lambda_k2, diff_subln, na_rpb, peer_w_query, peer_sub_keys,
           peer_u, peer_v):
    b, s, d = x.shape
    t = b * s
    depth = w_in.shape[0]
    rows = s // GRID_W
    kr = min(NA_KR_MAX, rows)
    cos_t, sin_up_t, sin_dn_t = _rope_tables(s)
    h = x.reshape(t, d)
    row = lambda v: v.reshape(1, -1).astype(F32)
    for l in range(depth):
        lam_init = 0.8 - 0.6 * math.exp(-0.3 * l)
        qkv, gates = _inproj(h, row(norm_mix[l]), w_in[l].astype(BF16), cos_t, sin_up_t,
                             sin_dn_t, s)
        qkv3 = qkv.reshape(b, s, QKV_W)
        ya = _diff_attn(qkv3, row(lambda_q1[l]), row(lambda_k1[l]), row(lambda_q2[l]),
                        row(lambda_k2[l]), row(diff_subln[l]), lam_init)
        yb = _na_attn(qkv3, _na_bias_table(na_rpb[l], rows, kr), rows, kr)
        keys = peer_sub_keys[l].reshape(2 * PEER_HEADS, PEER_N_KEYS, PEER_QUERY_DIM // 2)
        h_mid, xnt, scores_t = _merge(
            ya.reshape(t, A_V_W), yb.reshape(t, B_W), gates, h,
            w_proj_a[l].astype(BF16), w_proj_b[l].astype(BF16), w_out[l].astype(BF16),
            row(norm_ffn[l]), peer_w_query[l].astype(BF16), keys.astype(BF16))
        cnt, ea, rank2, eb = _select(scores_t)
        vt_blocks = peer_v[l].reshape(-1, EB_PEER, d).transpose(0, 2, 1).astype(BF16)
        h = _peer_mix(xnt, peer_u[l].astype(BF16), vt_blocks, cnt, ea, rank2, eb,
                      h_mid, row(norm_final), final_norm=(l == depth - 1))
    return h.reshape(b, s, d)
```

```python
import functools
import math

import jax
import jax.numpy as jnp
import numpy as np
from jax import lax
from jax.experimental import pallas as pl
from jax.experimental.pallas import tpu as pltpu

F32 = jnp.float32
BF16 = jnp.bfloat16

D_MODEL = 1024
GRID_W = 64
N_DIFF_HEADS = 4
DIFF_HEAD_DIM = 64
DIFF_V_DIM = 2 * DIFF_HEAD_DIM
ROPE_THETA = 500000.0
ROPE_DIM = DIFF_HEAD_DIM // 4
N_NA_HEADS = 8
NA_HEAD_DIM = 64
NA_KR_MAX = 8
NA_KC = 16
A_QK_W = N_DIFF_HEADS * 2 * DIFF_HEAD_DIM
A_V_W = N_DIFF_HEADS * DIFF_V_DIM
B_W = N_NA_HEADS * NA_HEAD_DIM
QKV_W = 2 * A_QK_W + A_V_W + 3 * B_W
GATE_W = 2 * D_MODEL
PEER_HEADS = 8
PEER_N_KEYS = 128
PEER_N_EXPERTS = PEER_N_KEYS * PEER_N_KEYS
PEER_QUERY_DIM = 256
PEER_TOPK = 16
EPS = 1e-6

LANES = 128
NEG_BIG = -1e30
MIB = 1024 * 1024

TM_INPROJ = 512
TQ_DIFF = 512
TM_MERGE = 512
NB_NA = 4
TT_PEER = 512
EB_PEER = 2048
SB_PEER = 512
GATE_ROWS = 2
GATE_DTYPE = F32
GATE_VREG_ROWS = 8
BF16_SUBLANE_ROWS = 16
TAB_EB_ROW = PEER_N_KEYS + GATE_VREG_ROWS
TAB_ROWS = TAB_EB_ROW + PEER_N_KEYS


def _cparams(sem, vmem_mib):
    return pltpu.CompilerParams(dimension_semantics=sem, vmem_limit_bytes=vmem_mib * MIB)


def _inproj_kernel(x_ref, g_ref, w_ref, c_ref, sa_ref, sb_ref, qkv_ref, gate_ref):
    x = x_ref[...]
    ms = jnp.mean(x * x, axis=-1, keepdims=True)
    n = (x * lax.rsqrt(ms + EPS) * g_ref[...]).astype(BF16)
    chunk = 512
    reps = chunk // LANES
    cos = jnp.tile(c_ref[...], (1, reps))
    sin_up = jnp.tile(sa_ref[...], (1, reps))
    sin_dn = jnp.tile(sb_ref[...], (1, reps))
    half = ROPE_DIM // 2
    for j in range(QKV_W // chunk):
        y = jnp.dot(n, w_ref[:, j * chunk:(j + 1) * chunk], preferred_element_type=F32)
        if j < 2 * A_QK_W // chunk:
            y = (y * cos + pltpu.roll(y, half, axis=1) * sin_up
                 + pltpu.roll(y, chunk - half, axis=1) * sin_dn)
            if j < A_QK_W // chunk:
                y = y * (DIFF_HEAD_DIM ** -0.5)
        qkv_ref[:, j * chunk:(j + 1) * chunk] = y.astype(BF16)
    for j in range(GATE_W // chunk):
        c0 = QKV_W + j * chunk
        gate_ref[:, j * chunk:(j + 1) * chunk] = jnp.dot(
            n, w_ref[:, c0:c0 + chunk], preferred_element_type=F32)


def _inproj(x2, g, w_bf, cos_t, sin_up_t, sin_dn_t, seq):
    t = x2.shape[0]
    tm = TM_INPROJ
    nblk = seq // tm
    in_w = w_bf.shape[1]
    tab_spec = pl.BlockSpec((tm, LANES), lambda i: (i % nblk, 0))
    return pl.pallas_call(
        _inproj_kernel,
        grid=(t // tm,),
        in_specs=[
            pl.BlockSpec((tm, D_MODEL), lambda i: (i, 0)),
            pl.BlockSpec((1, D_MODEL), lambda i: (0, 0)),
            pl.BlockSpec((D_MODEL, in_w), lambda i: (0, 0)),
            tab_spec, tab_spec, tab_spec,
        ],
        out_specs=[
            pl.BlockSpec((tm, QKV_W), lambda i: (i, 0)),
            pl.BlockSpec((tm, GATE_W), lambda i: (i, 0)),
        ],
        out_shape=[
            jax.ShapeDtypeStruct((t, QKV_W), BF16),
            jax.ShapeDtypeStruct((t, GATE_W), F32),
        ],
        compiler_params=_cparams(("arbitrary",), 56),
        name="inproj",
    )(x2, g, w_bf, cos_t, sin_up_t, sin_dn_t)


def _diff_attn_kernel(lam_init, q_ref, k_ref, v_ref, lq1_ref, lk1_ref, lq2_ref, lk2_ref,
                      g_ref, o_ref):
    q = q_ref[...]
    k = k_ref[...]
    lane = lax.broadcasted_iota(jnp.int32, q.shape, 1)
    zero = jnp.zeros_like(q)
    q1 = jnp.where(lane < DIFF_HEAD_DIM, q, zero)
    q2 = jnp.where(lane >= DIFF_HEAD_DIM, q, zero)
    nt = (((1,), (1,)), ((), ()))
    s1 = lax.dot_general(q1, k, nt, preferred_element_type=F32)
    s2 = lax.dot_general(q2, k, nt, preferred_element_type=F32)
    e1 = jnp.exp(s1 - jnp.max(s1, axis=-1, keepdims=True))
    e2 = jnp.exp(s2 - jnp.max(s2, axis=-1, keepdims=True))
    l1 = jnp.sum(e1, axis=-1, keepdims=True)
    l2 = jnp.sum(e2, axis=-1, keepdims=True)
    lam = (jnp.exp(jnp.sum(lq1_ref[...] * lk1_ref[...], axis=-1, keepdims=True))
           - jnp.exp(jnp.sum(lq2_ref[...] * lk2_ref[...], axis=-1, keepdims=True))
           + lam_init)
    v = v_ref[...]
    o = (jnp.dot(e1.astype(BF16), v, preferred_element_type=F32) * (1.0 / l1)
         - jnp.dot(e2.astype(BF16), v, preferred_element_type=F32) * (lam / l2))
    ms = jnp.mean(o * o, axis=-1, keepdims=True)
    o = o * lax.rsqrt(ms + EPS) * g_ref[...] * (1.0 - lam_init)
    o_ref[...] = o.astype(o_ref.dtype)


def _diff_attn(qkv3, lq1, lk1, lq2, lk2, subln_g, lam_init):
    b, s, _ = qkv3.shape
    tq = TQ_DIFF
    hw = 2 * DIFF_HEAD_DIM
    k_blk0 = A_QK_W // hw
    v_blk0 = 2 * A_QK_W // hw
    vec = pl.BlockSpec((1, DIFF_HEAD_DIM), lambda bi, h, i: (0, 0))
    return pl.pallas_call(
        functools.partial(_diff_attn_kernel, lam_init),
        grid=(b, N_DIFF_HEADS, s // tq),
        in_specs=[
            pl.BlockSpec((None, tq, hw), lambda bi, h, i: (bi, i, h)),
            pl.BlockSpec((None, s, hw), lambda bi, h, i: (bi, 0, k_blk0 + h)),
            pl.BlockSpec((None, s, DIFF_V_DIM), lambda bi, h, i: (bi, 0, v_blk0 + h)),
            vec, vec, vec, vec,
            pl.BlockSpec((1, DIFF_V_DIM), lambda bi, h, i: (0, 0)),
        ],
        out_specs=pl.BlockSpec((None, tq, DIFF_V_DIM), lambda bi, h, i: (bi, i, h)),
        out_shape=jax.ShapeDtypeStruct((b, s, A_V_W), BF16),
        compiler_params=_cparams(("arbitrary", "arbitrary", "arbitrary"), 48),
        name="diff_attn",
    )(qkv3, qkv3, qkv3, lq1, lk1, lq2, lk2, subln_g)


def _na_window_start(r, rows, kr):
    return jnp.clip(r - kr // 2, 0, rows - kr)


def _na_kernel(rows, kr, q_ref, k_ref, v_ref, bias_ref, o_ref):
    r = pl.program_id(1)
    start = pl.multiple_of(_na_window_start(r, rows, kr) * GRID_W, GRID_W)
    nkeys = kr * GRID_W
    nb = q_ref.shape[0]
    nt = (((1,), (1,)), ((), ()))
    pair_w = 2 * NA_HEAD_DIM
    lane = lax.broadcasted_iota(jnp.int32, (GRID_W, pair_w), 1)
    scores, vwins = [], []
    for bb in range(nb):
        kwin = k_ref[bb, pl.ds(start, nkeys), :]
        vwins.append(v_ref[bb, pl.ds(start, nkeys), :])
        q = q_ref[bb] * (NA_HEAD_DIM ** -0.5)
        for h in range(N_NA_HEADS):
            sl = slice((h // 2) * pair_w, (h // 2 + 1) * pair_w)
            keep = (lane < NA_HEAD_DIM) if h % 2 == 0 else (lane >= NA_HEAD_DIM)
            qm = jnp.where(keep, q[:, sl], jnp.zeros_like(q[:, sl]))
            s = lax.dot_general(qm, kwin[:, sl], nt, preferred_element_type=F32)
            scores.append(s + bias_ref[h])
    s = jnp.concatenate(scores, axis=0)
    e = jnp.exp(s - jnp.max(s, axis=-1, keepdims=True))
    p = (e * (1.0 / jnp.sum(e, axis=-1, keepdims=True))).astype(BF16)
    for bb in range(nb):
        for hp in range(N_NA_HEADS // 2):
            sl = slice(hp * pair_w, (hp + 1) * pair_w)
            row0 = (bb * N_NA_HEADS + 2 * hp) * GRID_W
            outs = [jnp.dot(p[row0 + hh * GRID_W:row0 + (hh + 1) * GRID_W], vwins[bb][:, sl],
                            preferred_element_type=F32) for hh in range(2)]
            o_ref[bb, :, sl] = jnp.where(lane < NA_HEAD_DIM, outs[0], outs[1]).astype(o_ref.dtype)


def _na_attn(qkv3, bias_tab, rows, kr):
    b, s, _ = qkv3.shape
    q_blk = (2 * A_QK_W + A_V_W) // B_W
    nkeys = kr * GRID_W
    nb = NB_NA if b % NB_NA == 0 else 1

    def bias_idx(bi, r):
        return (0, r - _na_window_start(r, rows, kr), 0, 0)

    return pl.pallas_call(
        functools.partial(_na_kernel, rows, kr),
        grid=(b // nb, rows),
        in_specs=[
            pl.BlockSpec((nb, GRID_W, B_W), lambda bi, r: (bi, r, q_blk)),
            pl.BlockSpec((nb, s, B_W), lambda bi, r: (bi, 0, q_blk + 1)),
            pl.BlockSpec((nb, s, B_W), lambda bi, r: (bi, 0, q_blk + 2)),
            pl.BlockSpec((N_NA_HEADS, None, GRID_W, nkeys), bias_idx),
        ],
        out_specs=pl.BlockSpec((nb, GRID_W, B_W), lambda bi, r: (bi, r, 0)),
        out_shape=jax.ShapeDtypeStruct((b, s, B_W), BF16),
        compiler_params=_cparams(("arbitrary", "arbitrary"), 56),
        name="na_attn",
    )(qkv3, qkv3, qkv3, bias_tab)


def _na_bias_table(rpb, rows, kr):
    w = np.arange(GRID_W)[:, None]
    c = np.arange(GRID_W)[None, :]
    cs = np.clip(w - NA_KC // 2, 0, GRID_W - NA_KC)
    valid = (c >= cs) & (c < cs + NA_KC)
    ncol = 2 * NA_KC - 1
    onehot = ((c - w + (NA_KC - 1))[None] == np.arange(ncol)[:, None, None]) & valid[None]
    top = NA_KR_MAX - 1
    rsel = jnp.stack([rpb[:, top - d:top - d + kr, :] for d in range(kr)], axis=1).astype(F32)
    tab = jnp.einsum('hdik,kwc->hdwic', rsel, jnp.asarray(onehot, F32),
                     precision=lax.Precision.HIGHEST)
    tab = jnp.where(valid[None, None, :, None, :], tab, NEG_BIG)
    return tab.reshape(rpb.shape[0], kr, GRID_W, kr * GRID_W)


def _merge_kernel(ya_ref, yb_ref, gate_ref, x_ref, wpa_ref, wpb_ref, wout_ref, g_ref, wq_ref,
                  keys_ref, h_ref, xnt_ref, st_ref):
    ma = jnp.dot(ya_ref[...], wpa_ref[...], preferred_element_type=F32)
    mb = jnp.dot(yb_ref[...], wpb_ref[...], preferred_element_type=F32)
    ga = gate_ref[:, :D_MODEL]
    gb = gate_ref[:, D_MODEL:]
    merged = jax.nn.sigmoid(ga) * ma + jax.nn.sigmoid(gb) * mb
    h = x_ref[...] + jnp.dot(merged.astype(BF16), wout_ref[...], preferred_element_type=F32)
    h_ref[...] = h
    ms = jnp.mean(h * h, axis=-1, keepdims=True)
    xn_f32 = h * lax.rsqrt(ms + EPS) * g_ref[...]
    xn = xn_f32.astype(BF16)
    xnt_ref[...] = xn_f32.T.astype(BF16)
    q = jnp.dot(xn, wq_ref[...], preferred_element_type=F32).astype(BF16)
    nt = (((1,), (1,)), ((), ()))
    half = PEER_QUERY_DIM // 2
    for hp in range(2 * PEER_HEADS):
        st_ref[hp % 2, :, hp // 2, :] = lax.dot_general(
            keys_ref[hp], q[:, hp * half:(hp + 1) * half], nt, preferred_element_type=F32)


def _merge(ya, yb, gates, x2, wpa, wpb, wout, g_ffn, wq, keys):
    t = x2.shape[0]
    tm = TM_MERGE
    qw = wq.shape[1]
    nlist = keys.shape[0]
    full = lambda shape: pl.BlockSpec(shape, lambda i: (0,) * len(shape))
    return pl.pallas_call(
        _merge_kernel,
        grid=(t // tm,),
        in_specs=[
            pl.BlockSpec((tm, A_V_W), lambda i: (i, 0)),
            pl.BlockSpec((tm, B_W), lambda i: (i, 0)),
            pl.BlockSpec((tm, GATE_W), lambda i: (i, 0)),
            pl.BlockSpec((tm, D_MODEL), lambda i: (i, 0)),
            full((A_V_W, D_MODEL)), full((B_W, D_MODEL)), full((D_MODEL, D_MODEL)),
            full((1, D_MODEL)), full((D_MODEL, qw)), full(keys.shape),
        ],
        out_specs=[
            pl.BlockSpec((tm, D_MODEL), lambda i: (i, 0)),
            pl.BlockSpec((D_MODEL, tm), lambda i: (0, i)),
            pl.BlockSpec((2, PEER_N_KEYS, nlist // 2, tm), lambda i: (0, 0, 0, i)),
        ],
        out_shape=[
            jax.ShapeDtypeStruct((t, D_MODEL), F32),
            jax.ShapeDtypeStruct((D_MODEL, t), BF16),
            jax.ShapeDtypeStruct((2, PEER_N_KEYS, nlist // 2, t), F32),
        ],
        compiler_params=_cparams(("arbitrary",), 56),
        name="merge_query",
    )(ya, yb, gates, x2, wpa, wpb, wout, g_ffn, wq, keys)


def _staircase_cells():
    k = PEER_TOPK
    return [(r1, r2) for r1 in range(k) for r2 in range(k // (r1 + 1))]


def _first_max(items):
    level = list(items)
    while len(level) > 1:
        nxt = []
        for i in range(0, len(level) - 1, 2):
            (v0, i0), (v1, i1) = level[i], level[i + 1]
            later = v1 > v0
            nxt.append((jnp.where(later, v1, v0), jnp.where(later, i1, i0)))
        if len(level) % 2:
            nxt.append(level[-1])
        level = nxt
    return level[0]


def _sort_desc(vals):
    vals = list(vals)
    n = len(vals)
    size = 2
    while size <= n:
        stride = size // 2
        while stride >= 1:
            for i in range(n):
                j = i ^ stride
                if j > i:
                    hi, lo = jnp.maximum(vals[i], vals[j]), jnp.minimum(vals[i], vals[j])
                    vals[i], vals[j] = (hi, lo) if (i & size) == 0 else (lo, hi)
            stride //= 2
        size *= 2
    return vals


def _merge_top(a, b):
    n = len(a)
    vals = [jnp.maximum(a[i], b[n - 1 - i]) for i in range(n)]
    stride = n // 2
    while stride >= 1:
        for i in range(n):
            j = i ^ stride
            if j > i:
                vals[i], vals[j] = jnp.maximum(vals[i], vals[j]), jnp.minimum(vals[i], vals[j])
        stride //= 2
    return vals


def _top_sorted(vals, k):
    groups = [_sort_desc(vals[i:i + k]) for i in range(0, len(vals), k)]
    while len(groups) > 1:
        groups = [_merge_top(groups[i], groups[i + 1]) for i in range(0, len(groups), 2)]
    return groups[0]


def _staircase(as_, bs_, cand_ref, pick_ref):
    k = PEER_TOPK
    tile = as_[0].shape
    cells = _staircase_cells()
    for ci, (r1, r2) in enumerate(cells):
        cand_ref[ci] = as_[r1] + bs_[r2]
    top = as_[0] + bs_[0]

    def pick(it, carry):
        prev, z = carry
        items = []
        for ci, (r1, r2) in enumerate(cells):
            pos = r1 * k + r2
            c = jnp.where(prev == pos, -jnp.inf, cand_ref[ci])
            cand_ref[ci] = c
            items.append((c, pos))
        m, first = _first_max(items)
        pick_ref[pl.ds(it, 1)] = first[None]
        return first, z + jnp.exp(m - top)

    _, z = lax.fori_loop(0, k, pick, (jnp.full(tile, -1, jnp.int32), jnp.zeros(tile, F32)))
    picked_row = [pick_ref[it] // k for it in range(k)]
    counts = []
    for r1 in range(k):
        c = jnp.zeros(tile, F32)
        for it in range(k):
            c = c + jnp.where(picked_row[it] == r1, 1.0, 0.0)
        counts.append(c)
    return counts, z


def _select_kernel(s_ref, cnt_ref, ea_ref, rank2_ref, eb_ref, cur_ref, cand_ref, srt_ref,
                   first_ref, pick_ref):
    k = PEER_TOPK
    nk = s_ref.shape[1]
    tile = s_ref.shape[2:]

    as_ = _top_sorted([s_ref[0, key] for key in range(nk)], k)
    bs_ = _top_sorted([s_ref[1, key] for key in range(nk)], k)
    counts, z = _staircase(as_, bs_, cand_ref, pick_ref)
    inv_z = 1.0 / z
    n_top = [jnp.zeros(tile, F32), jnp.zeros(tile, F32)]
    for key in range(nk):
        a, b = s_ref[0, key], s_ref[1, key]
        cnt = jnp.zeros(tile, F32)
        rank = jnp.full(tile, float(nk), F32)
        for r in range(k - 1, -1, -1):
            cnt = jnp.where(a >= as_[r], counts[r], cnt)
            rank = jnp.where(b >= bs_[r], float(r), rank)
        n_top[0] = n_top[0] + jnp.where(a >= as_[k - 1], 1.0, 0.0)
        n_top[1] = n_top[1] + jnp.where(b >= bs_[k - 1], 1.0, 0.0)
        cnt_ref[key] = cnt
        rank2_ref[:, key, :] = rank
        ea_ref[key] = jnp.exp(a - as_[0])
        eb_ref[:, key, :] = jnp.exp(b - bs_[0]) * inv_z

    tied = jnp.zeros(tile, F32)
    for srt, n in ((as_, n_top[0]), (bs_, n_top[1])):
        tied = jnp.where(n != float(k), 1.0, tied)
        for r in range(k - 1):
            tied = jnp.where(srt[r] == srt[r + 1], 1.0, tied)
    any_tied = jnp.max(tied)

    @pl.when(any_tied > 0.0)
    def _():
        _select_exact(s_ref, cnt_ref, ea_ref, rank2_ref, eb_ref, cur_ref, cand_ref, srt_ref,
                      first_ref, pick_ref)


def _select_exact(s_ref, cnt_ref, ea_ref, rank2_ref, eb_ref, cur_ref, cand_ref, srt_ref,
                  first_ref, pick_ref):
    k = PEER_TOPK
    nk = s_ref.shape[1]
    tile = s_ref.shape[2:]
    none = jnp.full(tile, -1, jnp.int32)

    cur_ref[...] = s_ref[...]

    def extract(it, prev):
        winners = []
        for l in range(2):
            items = []
            for key in range(nk):
                c = jnp.where(prev[l] == key, -jnp.inf, cur_ref[l, key])
                cur_ref[l, key] = c
                items.append((c, key))
            m, first = _first_max(items)
            srt_ref[l, pl.ds(it, 1)] = m[None]
            first_ref[l, pl.ds(it, 1)] = first[None]
            winners.append(first)
        return tuple(winners)

    lax.fori_loop(0, k, extract, (none, none))
    as_ = [srt_ref[0, r] for r in range(k)]
    bs_ = [srt_ref[1, r] for r in range(k)]
    counts, z = _staircase(as_, bs_, cand_ref, pick_ref)

    first_a = [first_ref[0, r] for r in range(k)]
    first_b = [first_ref[1, r] for r in range(k)]
    inv_z = 1.0 / z
    for key in range(nk):
        cnt = jnp.zeros(tile, F32)
        rank = jnp.full(tile, float(nk), F32)
        for r in range(k):
            cnt = jnp.where(first_a[r] == key, counts[r], cnt)
            rank = jnp.where(first_b[r] == key, float(r), rank)
        cnt_ref[key] = cnt
        rank2_ref[:, key, :] = rank
        ea_ref[key] = jnp.exp(s_ref[0, key] - as_[0])
        eb_ref[:, key, :] = jnp.exp(s_ref[1, key] - bs_[0]) * inv_z


def _select(scores):
    _, nk, heads, t = scores.shape
    k = PEER_TOPK
    tab = pl.BlockSpec((nk, heads, LANES), lambda i: (0, 0, i))
    shape = jax.ShapeDtypeStruct((nk, heads, t), F32)
    htab = pl.BlockSpec((heads, nk, LANES), lambda i: (0, 0, i))
    hshape = jax.ShapeDtypeStruct((heads, nk, t), F32)
    return pl.pallas_call(
        _select_kernel,
        grid=(t // LANES,),
        in_specs=[pl.BlockSpec((2, nk, heads, LANES), lambda i: (0, 0, 0, i))],
        out_specs=[tab, tab, htab, htab],
        out_shape=[shape, shape, hshape, hshape],
        scratch_shapes=[
            pltpu.VMEM((2, nk, heads, LANES), F32),
            pltpu.VMEM((len(_staircase_cells()), heads, LANES), F32),
            pltpu.VMEM((2, k, heads, LANES), F32),
            pltpu.VMEM((2, k, heads, LANES), jnp.int32),
            pltpu.VMEM((k, heads, LANES), jnp.int32),
        ],
        compiler_params=_cparams(("arbitrary",), 32),
        name="peer_select",
    )(scores)


def _peer_gate_chunk(tc, j_rows, cnt_ref, ea_ref, tab_ref, at_ref, p_ref):
    cols = slice(tc * LANES, (tc + 1) * LANES)
    vr = GATE_VREG_ROWS
    n_k = PEER_N_KEYS // vr
    acc = [[None] * n_k for _ in j_rows]
    for h in range(PEER_HEADS):
        bcast = lambda ref, j: jnp.broadcast_to(ref[j, h:h + 1, cols], (vr, LANES)).astype(GATE_DTYPE)
        cnt_b = [bcast(cnt_ref, j) for j in j_rows]
        ea_b = [bcast(ea_ref, j) for j in j_rows]
        for k in range(n_k):
            r2 = tab_ref[h, tc, k * vr:(k + 1) * vr, :]
            ebh = tab_ref[h, tc, TAB_EB_ROW + k * vr:TAB_EB_ROW + (k + 1) * vr, :]
            for jj in range(len(j_rows)):
                term = jnp.where(r2 < cnt_b[jj], ebh, 0.0) * ea_b[jj]
                acc[jj][k] = term if acc[jj][k] is None else acc[jj][k] + term
    pack = BF16_SUBLANE_ROWS // vr
    for jj, j in enumerate(j_rows):
        for m in range(n_k // pack):
            r0 = j * PEER_N_KEYS + m * BF16_SUBLANE_ROWS
            rows = slice(r0, r0 + BF16_SUBLANE_ROWS)
            act = at_ref[rows, cols]
            gelu = 0.5 * act * (1.0 + lax.erf(act * math.sqrt(0.5)))
            w = jnp.concatenate(acc[jj][m * pack:(m + 1) * pack], axis=0)
            p_ref[rows, cols] = (w * gelu.astype(GATE_DTYPE)).astype(BF16)


def _peer_mix_kernel(final_norm, xnt_ref, u_ref, vt_ref, cnt_ref, ea_ref, rank2_ref, eb_ref,
                     h_ref, g_ref, o_ref, at_ref, p_ref, acc_ref, tab_ref):
    e = pl.program_id(1)
    eb, tt = u_ref.shape[0], xnt_ref.shape[1]

    @pl.when(e == 0)
    def _():
        acc_ref[...] = jnp.zeros_like(acc_ref)
        for h in range(PEER_HEADS):
            for tc in range(tt // LANES):
                cols = slice(tc * LANES, (tc + 1) * LANES)
                tab_ref[h, tc, 0:PEER_N_KEYS, :] = rank2_ref[h, :, cols].astype(GATE_DTYPE)
                tab_ref[h, tc, TAB_EB_ROW:TAB_EB_ROW + PEER_N_KEYS, :] = eb_ref[h, :, cols].astype(GATE_DTYPE)

    rows_per_sub = SB_PEER // PEER_N_KEYS
    for sb in range(eb // SB_PEER):
        sub = slice(sb * SB_PEER, (sb + 1) * SB_PEER)
        at_ref[sub, 0:tt] = jnp.dot(u_ref[sub, :], xnt_ref[...], preferred_element_type=F32)
        for jg in range(sb * rows_per_sub, (sb + 1) * rows_per_sub, GATE_ROWS):
            for tc in range(tt // LANES):
                _peer_gate_chunk(tc, tuple(range(jg, jg + GATE_ROWS)), cnt_ref, ea_ref, tab_ref,
                                 at_ref, p_ref)
    acc_ref[...] += jnp.dot(vt_ref[...], p_ref[:, 0:tt], preferred_element_type=F32)

    @pl.when(e == pl.num_programs(1) - 1)
    def _():
        y = h_ref[...] + acc_ref[...].T
        if final_norm:
            ms = jnp.mean(y * y, axis=-1, keepdims=True)
            y = y * lax.rsqrt(ms + EPS) * g_ref[...]
        o_ref[...] = y


def _peer_mix(xnt, u_bf, vt_bf, cnt, ea, rank2, eb, h, g_final, final_norm):
    t = xnt.shape[1]
    tt, ebk = TT_PEER, EB_PEER
    n_exp = u_bf.shape[0]
    nk, heads, _ = cnt.shape
    tab = pl.BlockSpec((nk, heads, tt), lambda ti, e: (0, 0, ti))
    row_tab = pl.BlockSpec((ebk // nk, heads, tt), lambda ti, e: (e, 0, ti))
    htab = pl.BlockSpec((heads, nk, tt), lambda ti, e: (0, 0, ti))
    return pl.pallas_call(
        functools.partial(_peer_mix_kernel, final_norm),
        grid=(t // tt, n_exp // ebk),
        in_specs=[
            pl.BlockSpec((D_MODEL, tt), lambda ti, e: (0, ti)),
            pl.BlockSpec((ebk, D_MODEL), lambda ti, e: (e, 0)),
            pl.BlockSpec((None, D_MODEL, ebk), lambda ti, e: (e, 0, 0)),
            row_tab, row_tab, htab, htab,
            pl.BlockSpec((tt, D_MODEL), lambda ti, e: (ti, 0)),
            pl.BlockSpec((1, D_MODEL), lambda ti, e: (0, 0)),
        ],
        out_specs=pl.BlockSpec((tt, D_MODEL), lambda ti, e: (ti, 0)),
        out_shape=jax.ShapeDtypeStruct((t, D_MODEL), F32),
        scratch_shapes=[
            pltpu.VMEM((ebk, tt + LANES), F32),
            pltpu.VMEM((ebk, tt + LANES), BF16),
            pltpu.VMEM((D_MODEL, tt), F32),
            pltpu.VMEM((heads, tt // LANES, TAB_ROWS, LANES), GATE_DTYPE),
        ],
        compiler_params=_cparams(("arbitrary", "arbitrary"), 56),
        name="peer_mix",
    )(xnt, u_bf, vt_bf, cnt, ea, rank2, eb, h, g_final)


def _rope_tables(seq):
    half = ROPE_DIM // 2
    inv_freq = ROPE_THETA ** (-jnp.arange(0, ROPE_DIM, 2, dtype=F32) / ROPE_DIM)
    ang = jnp.arange(seq, dtype=F32)[:, None] * inv_freq[None, :]
    cos, sin = jnp.cos(ang), jnp.sin(ang)
    pad = jnp.zeros((seq, DIFF_HEAD_DIM - ROPE_DIM), F32)
    c64 = jnp.concatenate([cos, cos, pad + 1.0], axis=1)
    up64 = jnp.concatenate([jnp.zeros_like(sin), sin, pad], axis=1)
    dn64 = jnp.concatenate([-sin, jnp.zeros_like(sin), pad], axis=1)
    rep = LANES // DIFF_HEAD_DIM
    return jnp.tile(c64, (1, rep)), jnp.tile(up64, (1, rep)), jnp.tile(dn64, (1, rep))


def kernel(x, w_in, w_proj_a, w_proj_b, w_out, norm_mix, norm_ffn, norm_final, lambda_q1,
           lambda_k1, lambda_q2, lambda_k2, diff_subln, na_rpb, peer_w_query, peer_sub_keys,
           peer_u, peer_v):
    b, s, d = x.shape
    t = b * s
    depth = w_in.shape[0]
    rows = s // GRID_W
    kr = min(NA_KR_MAX, rows)
    cos_t, sin_up_t, sin_dn_t = _rope_tables(s)
    h = x.reshape(t, d)
    row = lambda v: v.reshape(1, -1).astype(F32)
    for l in range(depth):
        lam_init = 0.8 - 0.6 * math.exp(-0.3 * l)
        qkv, gates = _inproj(h, row(norm_mix[l]), w_in[l].astype(BF16), cos_t, sin_up_t,
                             sin_dn_t, s)
        qkv3 = qkv.reshape(b, s, QKV_W)
        ya = _diff_attn(qkv3, row(lambda_q1[l]), row(lambda_k1[l]), row(lambda_q2[l]),
                        row(lambda_k2[l]), row(diff_subln[l]), lam_init)
        yb = _na_attn(qkv3, _na_bias_table(na_rpb[l], rows, kr), rows, kr)
        keys = peer_sub_keys[l].reshape(2 * PEER_HEADS, PEER_N_KEYS, PEER_QUERY_DIM // 2)
        h_mid, xnt, scores_t = _merge(
            ya.reshape(t, A_V_W), yb.reshape(t, B_W), gates, h,
            w_proj_a[l].astype(BF16), w_proj_b[l].astype(BF16), w_out[l].astype(BF16),
            row(norm_ffn[l]), peer_w_query[l].astype(BF16), keys.astype(BF16))
        cnt, ea, rank2, eb = _select(scores_t)
        vt_blocks = peer_v[l].reshape(-1, EB_PEER, d).transpose(0, 2, 1).astype(BF16)
        h = _peer_mix(xnt, peer_u[l].astype(BF16), vt_blocks, cnt, ea, rank2, eb,
                      h_mid, row(norm_final), final_norm=(l == depth - 1))
    return h.reshape(b, s, d)
```

```python
import functools
import math

import jax
import jax.numpy as jnp
import numpy as np
from jax import lax
from jax.experimental import pallas as pl
from jax.experimental.pallas import tpu as pltpu

F32 = jnp.float32
BF16 = jnp.bfloat16

D_MODEL = 1024
GRID_W = 64
N_DIFF_HEADS = 4
DIFF_HEAD_DIM = 64
DIFF_V_DIM = 2 * DIFF_HEAD_DIM
ROPE_THETA = 500000.0
ROPE_DIM = DIFF_HEAD_DIM // 4
N_NA_HEADS = 8
NA_HEAD_DIM = 64
NA_KR_MAX = 8
NA_KC = 16
A_QK_W = N_DIFF_HEADS * 2 * DIFF_HEAD_DIM
A_V_W = N_DIFF_HEADS * DIFF_V_DIM
B_W = N_NA_HEADS * NA_HEAD_DIM
QKV_W = 2 * A_QK_W + A_V_W + 3 * B_W
GATE_W = 2 * D_MODEL
PEER_HEADS = 8
PEER_N_KEYS = 128
PEER_N_EXPERTS = PEER_N_KEYS * PEER_N_KEYS
PEER_QUERY_DIM = 256
PEER_TOPK = 16
EPS = 1e-6

LANES = 128
NEG_BIG = -1e30
MIB = 1024 * 1024

TM_INPROJ = 512
TQ_DIFF = 1024
TM_MERGE = 512
NB_NA = 4
TT_PEER = 512
EB_PEER = 2048
SB_PEER = 512
GATE_ROWS = 2
GATE_DTYPE = F32
GATE_VREG_ROWS = 8
BF16_SUBLANE_ROWS = 16
TAB_EB_ROW = PEER_N_KEYS + GATE_VREG_ROWS
TAB_ROWS = TAB_EB_ROW + PEER_N_KEYS


def _cparams(sem, vmem_mib):
    return pltpu.CompilerParams(dimension_semantics=sem, vmem_limit_bytes=vmem_mib * MIB)


def _inproj_kernel(x_ref, g_ref, w_ref, c_ref, sa_ref, sb_ref, qkv_ref, gate_ref):
    x = x_ref[...]
    ms = jnp.mean(x * x, axis=-1, keepdims=True)
    n = (x * lax.rsqrt(ms + EPS) * g_ref[...]).astype(BF16)
    chunk = 512
    reps = chunk // LANES
    cos = jnp.tile(c_ref[...], (1, reps))
    sin_up = jnp.tile(sa_ref[...], (1, reps))
    sin_dn = jnp.tile(sb_ref[...], (1, reps))
    half = ROPE_DIM // 2
    for j in range(QKV_W // chunk):
        y = jnp.dot(n, w_ref[:, j * chunk:(j + 1) * chunk], preferred_element_type=F32)
        if j < 2 * A_QK_W // chunk:
            y = (y * cos + pltpu.roll(y, half, axis=1) * sin_up
                 + pltpu.roll(y, chunk - half, axis=1) * sin_dn)
            if j < A_QK_W // chunk:
                y = y * (DIFF_HEAD_DIM ** -0.5)
        qkv_ref[:, j * chunk:(j + 1) * chunk] = y.astype(BF16)
    for j in range(GATE_W // chunk):
        c0 = QKV_W + j * chunk
        gate_ref[:, j * chunk:(j + 1) * chunk] = jnp.dot(
            n, w_ref[:, c0:c0 + chunk], preferred_element_type=F32)


def _inproj(x2, g, w_bf, cos_t, sin_up_t, sin_dn_t, seq):
    t = x2.shape[0]
    tm = TM_INPROJ
    nblk = seq // tm
    in_w = w_bf.shape[1]
    tab_spec = pl.BlockSpec((tm, LANES), lambda i: (i % nblk, 0))
    return pl.pallas_call(
        _inproj_kernel,
        grid=(t // tm,),
        in_specs=[
            pl.BlockSpec((tm, D_MODEL), lambda i: (i, 0)),
            pl.BlockSpec((1, D_MODEL), lambda i: (0, 0)),
            pl.BlockSpec((D_MODEL, in_w), lambda i: (0, 0)),
            tab_spec, tab_spec, tab_spec,
        ],
        out_specs=[
            pl.BlockSpec((tm, QKV_W), lambda i: (i, 0)),
            pl.BlockSpec((tm, GATE_W), lambda i: (i, 0)),
        ],
        out_shape=[
            jax.ShapeDtypeStruct((t, QKV_W), BF16),
            jax.ShapeDtypeStruct((t, GATE_W), F32),
        ],
        compiler_params=_cparams(("arbitrary",), 56),
        name="inproj",
    )(x2, g, w_bf, cos_t, sin_up_t, sin_dn_t)


def _diff_attn_kernel(lam_init, q_ref, k_ref, v_ref, lq1_ref, lk1_ref, lq2_ref, lk2_ref,
                      g_ref, o_ref):
    q = q_ref[...]
    k = k_ref[...]
    lane = lax.broadcasted_iota(jnp.int32, q.shape, 1)
    zero = jnp.zeros_like(q)
    q1 = jnp.where(lane < DIFF_HEAD_DIM, q, zero)
    q2 = jnp.where(lane >= DIFF_HEAD_DIM, q, zero)
    nt = (((1,), (1,)), ((), ()))
    s1 = lax.dot_general(q1, k, nt, preferred_element_type=F32)
    s2 = lax.dot_general(q2, k, nt, preferred_element_type=F32)
    e1 = jnp.exp(s1 - jnp.max(s1, axis=-1, keepdims=True))
    e2 = jnp.exp(s2 - jnp.max(s2, axis=-1, keepdims=True))
    l1 = jnp.sum(e1, axis=-1, keepdims=True)
    l2 = jnp.sum(e2, axis=-1, keepdims=True)
    lam = (jnp.exp(jnp.sum(lq1_ref[...] * lk1_ref[...], axis=-1, keepdims=True))
           - jnp.exp(jnp.sum(lq2_ref[...] * lk2_ref[...], axis=-1, keepdims=True))
           + lam_init)
    v = v_ref[...]
    o = (jnp.dot(e1.astype(BF16), v, preferred_element_type=F32) * (1.0 / l1)
         - jnp.dot(e2.astype(BF16), v, preferred_element_type=F32) * (lam / l2))
    ms = jnp.mean(o * o, axis=-1, keepdims=True)
    o = o * lax.rsqrt(ms + EPS) * g_ref[...] * (1.0 - lam_init)
    o_ref[...] = o.astype(o_ref.dtype)


def _diff_attn(qkv3, lq1, lk1, lq2, lk2, subln_g, lam_init):
    b, s, _ = qkv3.shape
    tq = TQ_DIFF
    hw = 2 * DIFF_HEAD_DIM
    k_blk0 = A_QK_W // hw
    v_blk0 = 2 * A_QK_W // hw
    vec = pl.BlockSpec((1, DIFF_HEAD_DIM), lambda bi, h, i: (0, 0))
    return pl.pallas_call(
        functools.partial(_diff_attn_kernel, lam_init),
        grid=(b, N_DIFF_HEADS, s // tq),
        in_specs=[
            pl.BlockSpec((None, tq, hw), lambda bi, h, i: (bi, i, h)),
            pl.BlockSpec((None, s, hw), lambda bi, h, i: (bi, 0, k_blk0 + h)),
            pl.BlockSpec((None, s, DIFF_V_DIM), lambda bi, h, i: (bi, 0, v_blk0 + h)),
            vec, vec, vec, vec,
            pl.BlockSpec((1, DIFF_V_DIM), lambda bi, h, i: (0, 0)),
        ],
        out_specs=pl.BlockSpec((None, tq, DIFF_V_DIM), lambda bi, h, i: (bi, i, h)),
        out_shape=jax.ShapeDtypeStruct((b, s, A_V_W), BF16),
        compiler_params=_cparams(("arbitrary", "arbitrary", "arbitrary"), 48),
        name="diff_attn",
    )(qkv3, qkv3, qkv3, lq1, lk1, lq2, lk2, subln_g)


def _na_window_start(r, rows, kr):
    return jnp.clip(r - kr // 2, 0, rows - kr)


def _na_kernel(rows, kr, q_ref, k_ref, v_ref, bias_ref, o_ref):
    r = pl.program_id(1)
    start = pl.multiple_of(_na_window_start(r, rows, kr) * GRID_W, GRID_W)
    nkeys = kr * GRID_W
    nb = q_ref.shape[0]
    nt = (((1,), (1,)), ((), ()))
    pair_w = 2 * NA_HEAD_DIM
    lane = lax.broadcasted_iota(jnp.int32, (GRID_W, pair_w), 1)
    scores, vwins = [], []
    for bb in range(nb):
        kwin = k_ref[bb, pl.ds(start, nkeys), :]
        vwins.append(v_ref[bb, pl.ds(start, nkeys), :])
        q = q_ref[bb] * (NA_HEAD_DIM ** -0.5)
        for h in range(N_NA_HEADS):
            sl = slice((h // 2) * pair_w, (h // 2 + 1) * pair_w)
            keep = (lane < NA_HEAD_DIM) if h % 2 == 0 else (lane >= NA_HEAD_DIM)
            qm = jnp.where(keep, q[:, sl], jnp.zeros_like(q[:, sl]))
            s = lax.dot_general(qm, kwin[:, sl], nt, preferred_element_type=F32)
            scores.append(s + bias_ref[h])
    s = jnp.concatenate(scores, axis=0)
    e = jnp.exp(s - jnp.max(s, axis=-1, keepdims=True))
    p = (e * (1.0 / jnp.sum(e, axis=-1, keepdims=True))).astype(BF16)
    for bb in range(nb):
        for hp in range(N_NA_HEADS // 2):
            sl = slice(hp * pair_w, (hp + 1) * pair_w)
            row0 = (bb * N_NA_HEADS + 2 * hp) * GRID_W
            outs = [jnp.dot(p[row0 + hh * GRID_W:row0 + (hh + 1) * GRID_W], vwins[bb][:, sl],
                            preferred_element_type=F32) for hh in range(2)]
            o_ref[bb, :, sl] = jnp.where(lane < NA_HEAD_DIM, outs[0], outs[1]).astype(o_ref.dtype)


def _na_attn(qkv3, bias_tab, rows, kr):
    b, s, _ = qkv3.shape
    q_blk = (2 * A_QK_W + A_V_W) // B_W
    nkeys = kr * GRID_W
    nb = NB_NA if b % NB_NA == 0 else 1

    def bias_idx(bi, r):
        return (0, r - _na_window_start(r, rows, kr), 0, 0)

    return pl.pallas_call(
        functools.partial(_na_kernel, rows, kr),
        grid=(b // nb, rows),
        in_specs=[
            pl.BlockSpec((nb, GRID_W, B_W), lambda bi, r: (bi, r, q_blk)),
            pl.BlockSpec((nb, s, B_W), lambda bi, r: (bi, 0, q_blk + 1)),
            pl.BlockSpec((nb, s, B_W), lambda bi, r: (bi, 0, q_blk + 2)),
            pl.BlockSpec((N_NA_HEADS, None, GRID_W, nkeys), bias_idx),
        ],
        out_specs=pl.BlockSpec((nb, GRID_W, B_W), lambda bi, r: (bi, r, 0)),
        out_shape=jax.ShapeDtypeStruct((b, s, B_W), BF16),
        compiler_params=_cparams(("arbitrary", "arbitrary"), 56),
        name="na_attn",
    )(qkv3, qkv3, qkv3, bias_tab)


def _na_bias_table(rpb, rows, kr):
    w = np.arange(GRID_W)[:, None]
    c = np.arange(GRID_W)[None, :]
    cs = np.clip(w - NA_KC // 2, 0, GRID_W - NA_KC)
    valid = (c >= cs) & (c < cs + NA_KC)
    ncol = 2 * NA_KC - 1
    onehot = ((c - w + (NA_KC - 1))[None] == np.arange(ncol)[:, None, None]) & valid[None]
    top = NA_KR_MAX - 1
    rsel = jnp.stack([rpb[:, top - d:top - d + kr, :] for d in range(kr)], axis=1).astype(F32)
    tab = jnp.einsum('hdik,kwc->hdwic', rsel, jnp.asarray(onehot, F32),
                     precision=lax.Precision.HIGHEST)
    tab = jnp.where(valid[None, None, :, None, :], tab, NEG_BIG)
    return tab.reshape(rpb.shape[0], kr, GRID_W, kr * GRID_W)


def _merge_kernel(ya_ref, yb_ref, gate_ref, x_ref, wpa_ref, wpb_ref, wout_ref, g_ref, wq_ref,
                  keys_ref, h_ref, xnt_ref, st_ref):
    ma = jnp.dot(ya_ref[...], wpa_ref[...], preferred_element_type=F32)
    mb = jnp.dot(yb_ref[...], wpb_ref[...], preferred_element_type=F32)
    ga = gate_ref[:, :D_MODEL]
    gb = gate_ref[:, D_MODEL:]
    merged = jax.nn.sigmoid(ga) * ma + jax.nn.sigmoid(gb) * mb
    h = x_ref[...] + jnp.dot(merged.astype(BF16), wout_ref[...], preferred_element_type=F32)
    h_ref[...] = h
    ms = jnp.mean(h * h, axis=-1, keepdims=True)
    xn_f32 = h * lax.rsqrt(ms + EPS) * g_ref[...]
    xn = xn_f32.astype(BF16)
    xnt_ref[...] = xn_f32.T.astype(BF16)
    q = jnp.dot(xn, wq_ref[...], preferred_element_type=F32).astype(BF16)
    nt = (((1,), (1,)), ((), ()))
    half = PEER_QUERY_DIM // 2
    for hp in range(2 * PEER_HEADS):
        st_ref[hp % 2, :, hp // 2, :] = lax.dot_general(
            keys_ref[hp], q[:, hp * half:(hp + 1) * half], nt, preferred_element_type=F32)


def _merge(ya, yb, gates, x2, wpa, wpb, wout, g_ffn, wq, keys):
    t = x2.shape[0]
    tm = TM_MERGE
    qw = wq.shape[1]
    nlist = keys.shape[0]
    full = lambda shape: pl.BlockSpec(shape, lambda i: (0,) * len(shape))
    return pl.pallas_call(
        _merge_kernel,
        grid=(t // tm,),
        in_specs=[
            pl.BlockSpec((tm, A_V_W), lambda i: (i, 0)),
            pl.BlockSpec((tm, B_W), lambda i: (i, 0)),
            pl.BlockSpec((tm, GATE_W), lambda i: (i, 0)),
            pl.BlockSpec((tm, D_MODEL), lambda i: (i, 0)),
            full((A_V_W, D_MODEL)), full((B_W, D_MODEL)), full((D_MODEL, D_MODEL)),
            full((1, D_MODEL)), full((D_MODEL, qw)), full(keys.shape),
        ],
        out_specs=[
            pl.BlockSpec((tm, D_MODEL), lambda i: (i, 0)),
            pl.BlockSpec((D_MODEL, tm), lambda i: (0, i)),
            pl.BlockSpec((2, PEER_N_KEYS, nlist // 2, tm), lambda i: (0, 0, 0, i)),
        ],
        out_shape=[
            jax.ShapeDtypeStruct((t, D_MODEL), F32),
            jax.ShapeDtypeStruct((D_MODEL, t), BF16),
            jax.ShapeDtypeStruct((2, PEER_N_KEYS, nlist // 2, t), F32),
        ],
        compiler_params=_cparams(("arbitrary",), 56),
        name="merge_query",
    )(ya, yb, gates, x2, wpa, wpb, wout, g_ffn, wq, keys)


def _staircase_cells():
    k = PEER_TOPK
    return [(r1, r2) for r1 in range(k) for r2 in range(k // (r1 + 1))]


def _first_max(items):
    level = list(items)
    while len(level) > 1:
        nxt = []
        for i in range(0, len(level) - 1, 2):
            (v0, i0), (v1, i1) = level[i], level[i + 1]
            later = v1 > v0
            nxt.append((jnp.where(later, v1, v0), jnp.where(later, i1, i0)))
        if len(level) % 2:
            nxt.append(level[-1])
        level = nxt
    return level[0]


def _sort_desc(vals):
    vals = list(vals)
    n = len(vals)
    size = 2
    while size <= n:
        stride = size // 2
        while stride >= 1:
            for i in range(n):
                j = i ^ stride
                if j > i:
                    hi, lo = jnp.maximum(vals[i], vals[j]), jnp.minimum(vals[i], vals[j])
                    vals[i], vals[j] = (hi, lo) if (i & size) == 0 else (lo, hi)
            stride //= 2
        size *= 2
    return vals


def _merge_top(a, b):
    n = len(a)
    vals = [jnp.maximum(a[i], b[n - 1 - i]) for i in range(n)]
    stride = n // 2
    while stride >= 1:
        for i in range(n):
            j = i ^ stride
            if j > i:
                vals[i], vals[j] = jnp.maximum(vals[i], vals[j]), jnp.minimum(vals[i], vals[j])
        stride //= 2
    return vals


def _top_sorted(vals, k):
    groups = [_sort_desc(vals[i:i + k]) for i in range(0, len(vals), k)]
    while len(groups) > 1:
        groups = [_merge_top(groups[i], groups[i + 1]) for i in range(0, len(groups), 2)]
    return groups[0]


def _staircase(as_, bs_, cand_ref, pick_ref):
    k = PEER_TOPK
    tile = as_[0].shape
    cells = _staircase_cells()
    for ci, (r1, r2) in enumerate(cells):
        cand_ref[ci] = as_[r1] + bs_[r2]
    top = as_[0] + bs_[0]

    def pick(it, carry):
        prev, z = carry
        items = []
        for ci, (r1, r2) in enumerate(cells):
            pos = r1 * k + r2
            c = jnp.where(prev == pos, -jnp.inf, cand_ref[ci])
            cand_ref[ci] = c
            items.append((c, pos))
        m, first = _first_max(items)
        pick_ref[pl.ds(it, 1)] = first[None]
        return first, z + jnp.exp(m - top)

    _, z = lax.fori_loop(0, k, pick, (jnp.full(tile, -1, jnp.int32), jnp.zeros(tile, F32)))
    picked_row = [pick_ref[it] // k for it in range(k)]
    counts = []
    for r1 in range(k):
        c = jnp.zeros(tile, F32)
        for it in range(k):
            c = c + jnp.where(picked_row[it] == r1, 1.0, 0.0)
        counts.append(c)
    return counts, z


def _select_kernel(s_ref, cnt_ref, ea_ref, rank2_ref, eb_ref, cur_ref, cand_ref, srt_ref,
                   first_ref, pick_ref):
    k = PEER_TOPK
    nk = s_ref.shape[1]
    tile = s_ref.shape[2:]

    as_ = _top_sorted([s_ref[0, key] for key in range(nk)], k)
    bs_ = _top_sorted([s_ref[1, key] for key in range(nk)], k)
    counts, z = _staircase(as_, bs_, cand_ref, pick_ref)
    inv_z = 1.0 / z
    n_top = [jnp.zeros(tile, F32), jnp.zeros(tile, F32)]
    for key in range(nk):
        a, b = s_ref[0, key], s_ref[1, key]
        cnt = jnp.zeros(tile, F32)
        rank = jnp.full(tile, float(nk), F32)
        for r in range(k - 1, -1, -1):
            cnt = jnp.where(a >= as_[r], counts[r], cnt)
            rank = jnp.where(b >= bs_[r], float(r), rank)
        n_top[0] = n_top[0] + jnp.where(a >= as_[k - 1], 1.0, 0.0)
        n_top[1] = n_top[1] + jnp.where(b >= bs_[k - 1], 1.0, 0.0)
        cnt_ref[key] = cnt
        rank2_ref[:, key, :] = rank
        ea_ref[key] = jnp.exp(a - as_[0])
        eb_ref[:, key, :] = jnp.exp(b - bs_[0]) * inv_z

    tied = jnp.zeros(tile, F32)
    for srt, n in ((as_, n_top[0]), (bs_, n_top[1])):
        tied = jnp.where(n != float(k), 1.0, tied)
        for r in range(k - 1):
            tied = jnp.where(srt[r] == srt[r + 1], 1.0, tied)
    any_tied = jnp.max(tied)

    @pl.when(any_tied > 0.0)
    def _():
        _select_exact(s_ref, cnt_ref, ea_ref, rank2_ref, eb_ref, cur_ref, cand_ref, srt_ref,
                      first_ref, pick_ref)


def _select_exact(s_ref, cnt_ref, ea_ref, rank2_ref, eb_ref, cur_ref, cand_ref, srt_ref,
                  first_ref, pick_ref):
    k = PEER_TOPK
    nk = s_ref.shape[1]
    tile = s_ref.shape[2:]
    none = jnp.full(tile, -1, jnp.int32)

    cur_ref[...] = s_ref[...]

    def extract(it, prev):
        winners = []
        for l in range(2):
            items = []
            for key in range(nk):
                c = jnp.where(prev[l] == key, -jnp.inf, cur_ref[l, key])
                cur_ref[l, key] = c
                items.append((c, key))
            m, first = _first_max(items)
            srt_ref[l, pl.ds(it, 1)] = m[None]
            first_ref[l, pl.ds(it, 1)] = first[None]
            winners.append(first)
        return tuple(winners)

    lax.fori_loop(0, k, extract, (none, none))
    as_ = [srt_ref[0, r] for r in range(k)]
    bs_ = [srt_ref[1, r] for r in range(k)]
    counts, z = _staircase(as_, bs_, cand_ref, pick_ref)

    first_a = [first_ref[0, r] for r in range(k)]
    first_b = [first_ref[1, r] for r in range(k)]
    inv_z = 1.0 / z
    for key in range(nk):
        cnt = jnp.zeros(tile, F32)
        rank = jnp.full(tile, float(nk), F32)
        for r in range(k):
            cnt = jnp.where(first_a[r] == key, counts[r], cnt)
            rank = jnp.where(first_b[r] == key, float(r), rank)
        cnt_ref[key] = cnt
        rank2_ref[:, key, :] = rank
        ea_ref[key] = jnp.exp(s_ref[0, key] - as_[0])
        eb_ref[:, key, :] = jnp.exp(s_ref[1, key] - bs_[0]) * inv_z


def _select(scores):
    _, nk, heads, t = scores.shape
    k = PEER_TOPK
    tab = pl.BlockSpec((nk, heads, LANES), lambda i: (0, 0, i))
    shape = jax.ShapeDtypeStruct((nk, heads, t), F32)
    htab = pl.BlockSpec((heads, nk, LANES), lambda i: (0, 0, i))
    hshape = jax.ShapeDtypeStruct((heads, nk, t), F32)
    return pl.pallas_call(
        _select_kernel,
        grid=(t // LANES,),
        in_specs=[pl.BlockSpec((2, nk, heads, LANES), lambda i: (0, 0, 0, i))],
        out_specs=[tab, tab, htab, htab],
        out_shape=[shape, shape, hshape, hshape],
        scratch_shapes=[
            pltpu.VMEM((2, nk, heads, LANES), F32),
            pltpu.VMEM((len(_staircase_cells()), heads, LANES), F32),
            pltpu.VMEM((2, k, heads, LANES), F32),
            pltpu.VMEM((2, k, heads, LANES), jnp.int32),
            pltpu.VMEM((k, heads, LANES), jnp.int32),
        ],
        compiler_params=_cparams(("arbitrary",), 32),
        name="peer_select",
    )(scores)


def _peer_gate_chunk(tc, j_rows, cnt_ref, ea_ref, tab_ref, at_ref, p_ref):
    cols = slice(tc * LANES, (tc + 1) * LANES)
    vr = GATE_VREG_ROWS
    n_k = PEER_N_KEYS // vr
    acc = [[None] * n_k for _ in j_rows]
    for h in range(PEER_HEADS):
        bcast = lambda ref, j: jnp.broadcast_to(ref[j, h:h + 1, cols], (vr, LANES)).astype(GATE_DTYPE)
        cnt_b = [bcast(cnt_ref, j) for j in j_rows]
        ea_b = [bcast(ea_ref, j) for j in j_rows]
        for k in range(n_k):
            r2 = tab_ref[h, tc, k * vr:(k + 1) * vr, :]
            ebh = tab_ref[h, tc, TAB_EB_ROW + k * vr:TAB_EB_ROW + (k + 1) * vr, :]
            for jj in range(len(j_rows)):
                term = jnp.where(r2 < cnt_b[jj], ebh, 0.0) * ea_b[jj]
                acc[jj][k] = term if acc[jj][k] is None else acc[jj][k] + term
    pack = BF16_SUBLANE_ROWS // vr
    for jj, j in enumerate(j_rows):
        for m in range(n_k // pack):
            r0 = j * PEER_N_KEYS + m * BF16_SUBLANE_ROWS
            rows = slice(r0, r0 + BF16_SUBLANE_ROWS)
            act = at_ref[rows, cols]
            gelu = 0.5 * act * (1.0 + lax.erf(act * math.sqrt(0.5)))
            w = jnp.concatenate(acc[jj][m * pack:(m + 1) * pack], axis=0)
            p_ref[rows, cols] = (w * gelu.astype(GATE_DTYPE)).astype(BF16)


def _peer_mix_kernel(final_norm, xnt_ref, u_ref, vt_ref, cnt_ref, ea_ref, rank2_ref, eb_ref,
                     h_ref, g_ref, o_ref, at_ref, p_ref, acc_ref, tab_ref):
    e = pl.program_id(1)
    eb, tt = u_ref.shape[0], xnt_ref.shape[1]

    @pl.when(e == 0)
    def _():
        acc_ref[...] = jnp.zeros_like(acc_ref)
        for h in range(PEER_HEADS):
            for tc in range(tt // LANES):
                cols = slice(tc * LANES, (tc + 1) * LANES)
                tab_ref[h, tc, 0:PEER_N_KEYS, :] = rank2_ref[h, :, cols].astype(GATE_DTYPE)
                tab_ref[h, tc, TAB_EB_ROW:TAB_EB_ROW + PEER_N_KEYS, :] = eb_ref[h, :, cols].astype(GATE_DTYPE)

    rows_per_sub = SB_PEER // PEER_N_KEYS
    for sb in range(eb // SB_PEER):
        sub = slice(sb * SB_PEER, (sb + 1) * SB_PEER)
        at_ref[sub, 0:tt] = jnp.dot(u_ref[sub, :], xnt_ref[...], preferred_element_type=F32)
        for jg in range(sb * rows_per_sub, (sb + 1) * rows_per_sub, GATE_ROWS):
            for tc in range(tt // LANES):
                _peer_gate_chunk(tc, tuple(range(jg, jg + GATE_ROWS)), cnt_ref, ea_ref, tab_ref,
                                 at_ref, p_ref)
    acc_ref[...] += jnp.dot(vt_ref[...], p_ref[:, 0:tt], preferred_element_type=F32)

    @pl.when(e == pl.num_programs(1) - 1)
    def _():
        y = h_ref[...] + acc_ref[...].T
        if final_norm:
            ms = jnp.mean(y * y, axis=-1, keepdims=True)
            y = y * lax.rsqrt(ms + EPS) * g_ref[...]
        o_ref[...] = y


def _peer_mix(xnt, u_bf, vt_bf, cnt, ea, rank2, eb, h, g_final, final_norm):
    t = xnt.shape[1]
    tt, ebk = TT_PEER, EB_PEER
    n_exp = u_bf.shape[0]
    nk, heads, _ = cnt.shape
    tab = pl.BlockSpec((nk, heads, tt), lambda ti, e: (0, 0, ti))
    row_tab = pl.BlockSpec((ebk // nk, heads, tt), lambda ti, e: (e, 0, ti))
    htab = pl.BlockSpec((heads, nk, tt), lambda ti, e: (0, 0, ti))
    return pl.pallas_call(
        functools.partial(_peer_mix_kernel, final_norm),
        grid=(t // tt, n_exp // ebk),
        in_specs=[
            pl.BlockSpec((D_MODEL, tt), lambda ti, e: (0, ti)),
            pl.BlockSpec((ebk, D_MODEL), lambda ti, e: (e, 0)),
            pl.BlockSpec((None, D_MODEL, ebk), lambda ti, e: (e, 0, 0)),
            row_tab, row_tab, htab, htab,
            pl.BlockSpec((tt, D_MODEL), lambda ti, e: (ti, 0)),
            pl.BlockSpec((1, D_MODEL), lambda ti, e: (0, 0)),
        ],
        out_specs=pl.BlockSpec((tt, D_MODEL), lambda ti, e: (ti, 0)),
        out_shape=jax.ShapeDtypeStruct((t, D_MODEL), F32),
        scratch_shapes=[
            pltpu.VMEM((ebk, tt + LANES), F32),
            pltpu.VMEM((ebk, tt + LANES), BF16),
            pltpu.VMEM((D_MODEL, tt), F32),
            pltpu.VMEM((heads, tt // LANES, TAB_ROWS, LANES), GATE_DTYPE),
        ],
        compiler_params=_cparams(("arbitrary", "arbitrary"), 56),
        name="peer_mix",
    )(xnt, u_bf, vt_bf, cnt, ea, rank2, eb, h, g_final)


def _rope_tables(seq):
    half = ROPE_DIM // 2
    inv_freq = ROPE_THETA ** (-jnp.arange(0, ROPE_DIM, 2, dtype=F32) / ROPE_DIM)
    ang = jnp.arange(seq, dtype=F32)[:, None] * inv_freq[None, :]
    cos, sin = jnp.cos(ang), jnp.sin(ang)
    pad = jnp.zeros((seq, DIFF_HEAD_DIM - ROPE_DIM), F32)
    c64 = jnp.concatenate([cos, cos, pad + 1.0], axis=1)
    up64 = jnp.concatenate([jnp.zeros_like(sin), sin, pad], axis=1)
    dn64 = jnp.concatenate([-sin, jnp.zeros_like(sin), pad], axis=1)
    rep = LANES // DIFF_HEAD_DIM
    return jnp.tile(c64, (1, rep)), jnp.tile(up64, (1, rep)), jnp.tile(dn64, (1, rep))


def kernel(x, w_in, w_proj_a, w_proj_b, w_out, norm_mix, norm_ffn, norm_final, lambda_q1,
           lambda_k1, lambda_q2, lambda_k2, diff_subln, na_rpb, peer_w_query, peer_sub_keys,
           peer_u, peer_v):
    b, s, d = x.shape
    t = b * s
    depth = w_in.shape[0]
    rows = s // GRID_W
    kr = min(NA_KR_MAX, rows)
    cos_t, sin_up_t, sin_dn_t = _rope_tables(s)
    h = x.reshape(t, d)
    row = lambda v: v.reshape(1, -1).astype(F32)
    for l in range(depth):
        lam_init = 0.8 - 0.6 * math.exp(-0.3 * l)
        qkv, gates = _inproj(h, row(norm_mix[l]), w_in[l].astype(BF16), cos_t, sin_up_t,
                             sin_dn_t, s)
        qkv3 = qkv.reshape(b, s, QKV_W)
        ya = _diff_attn(qkv3, row(lambda_q1[l]), row(lambda_k1[l]), row(lambda_q2[l]),
                        row(lambda_k2[l]), row(diff_subln[l]), lam_init)
        yb = _na_attn(qkv3, _na_bias_table(na_rpb[l], rows, kr), rows, kr)
        keys = peer_sub_keys[l].reshape(2 * PEER_HEADS, PEER_N_KEYS, PEER_QUERY_DIM // 2)
        h_mid, xnt, scores_t = _merge(
            ya.reshape(t, A_V_W), yb.reshape(t, B_W), gates, h,
            w_proj_a[l].astype(BF16), w_proj_b[l].astype(BF16), w_out[l].astype(BF16),
            row(norm_ffn[l]), peer_w_query[l].astype(BF16), keys.astype(BF16))
        cnt, ea, rank2, eb = _select(scores_t)
        vt_blocks = peer_v[l].reshape(-1, EB_PEER, d).transpose(0, 2, 1).astype(BF16)
        h = _peer_mix(xnt, peer_u[l].astype(BF16), vt_blocks, cnt, ea, rank2, eb,
                      h_mid, row(norm_final), final_norm=(l == depth - 1))
    return h.reshape(b, s, d)
```

```python
import functools
import math

import jax
import jax.numpy as jnp
import numpy as np
from jax import lax
from jax.experimental import pallas as pl
from jax.experimental.pallas import tpu as pltpu

F32 = jnp.float32
BF16 = jnp.bfloat16

D_MODEL = 1024
GRID_W = 64
N_DIFF_HEADS = 4
DIFF_HEAD_DIM = 64
DIFF_V_DIM = 2 * DIFF_HEAD_DIM
ROPE_THETA = 500000.0
ROPE_DIM = DIFF_HEAD_DIM // 4
N_NA_HEADS = 8
NA_HEAD_DIM = 64
NA_KR_MAX = 8
NA_KC = 16
A_QK_W = N_DIFF_HEADS * 2 * DIFF_HEAD_DIM
A_V_W = N_DIFF_HEADS * DIFF_V_DIM
B_W = N_NA_HEADS * NA_HEAD_DIM
QKV_W = 2 * A_QK_W + A_V_W + 3 * B_W
GATE_W = 2 * D_MODEL
PEER_HEADS = 8
PEER_N_KEYS = 128
PEER_N_EXPERTS = PEER_N_KEYS * PEER_N_KEYS
PEER_QUERY_DIM = 256
PEER_TOPK = 16
EPS = 1e-6

LANES = 128
NEG_BIG = -1e30
MIB = 1024 * 1024

TM_INPROJ = 1024
TQ_DIFF = 1024
TM_MERGE = 512
NB_NA = 4
TT_PEER = 512
EB_PEER = 2048
SB_PEER = 512
GATE_ROWS = 2
GATE_DTYPE = F32
GATE_VREG_ROWS = 8
BF16_SUBLANE_ROWS = 16
TAB_EB_ROW = PEER_N_KEYS + GATE_VREG_ROWS
TAB_ROWS = TAB_EB_ROW + PEER_N_KEYS


def _cparams(sem, vmem_mib):
    return pltpu.CompilerParams(dimension_semantics=sem, vmem_limit_bytes=vmem_mib * MIB)


def _inproj_kernel(x_ref, g_ref, w_ref, c_ref, sa_ref, sb_ref, qkv_ref, gate_ref):
    x = x_ref[...]
    ms = jnp.mean(x * x, axis=-1, keepdims=True)
    n = (x * lax.rsqrt(ms + EPS) * g_ref[...]).astype(BF16)
    chunk = 512
    reps = chunk // LANES
    cos = jnp.tile(c_ref[...], (1, reps))
    sin_up = jnp.tile(sa_ref[...], (1, reps))
    sin_dn = jnp.tile(sb_ref[...], (1, reps))
    half = ROPE_DIM // 2
    for j in range(QKV_W // chunk):
        y = jnp.dot(n, w_ref[:, j * chunk:(j + 1) * chunk], preferred_element_type=F32)
        if j < 2 * A_QK_W // chunk:
            y = (y * cos + pltpu.roll(y, half, axis=1) * sin_up
                 + pltpu.roll(y, chunk - half, axis=1) * sin_dn)
            if j < A_QK_W // chunk:
                y = y * (DIFF_HEAD_DIM ** -0.5)
        qkv_ref[:, j * chunk:(j + 1) * chunk] = y.astype(BF16)
    for j in range(GATE_W // chunk):
        c0 = QKV_W + j * chunk
        gate_ref[:, j * chunk:(j + 1) * chunk] = jnp.dot(
            n, w_ref[:, c0:c0 + chunk], preferred_element_type=F32)


def _inproj(x2, g, w_bf, cos_t, sin_up_t, sin_dn_t, seq):
    t = x2.shape[0]
    tm = TM_INPROJ
    nblk = seq // tm
    in_w = w_bf.shape[1]
    tab_spec = pl.BlockSpec((tm, LANES), lambda i: (i % nblk, 0))
    return pl.pallas_call(
        _inproj_kernel,
        grid=(t // tm,),
        in_specs=[
            pl.BlockSpec((tm, D_MODEL), lambda i: (i, 0)),
            pl.BlockSpec((1, D_MODEL), lambda i: (0, 0)),
            pl.BlockSpec((D_MODEL, in_w), lambda i: (0, 0), pipeline_mode=pl.Buffered(1)),
            tab_spec, tab_spec, tab_spec,
        ],
        out_specs=[
            pl.BlockSpec((tm, QKV_W), lambda i: (i, 0)),
            pl.BlockSpec((tm, GATE_W), lambda i: (i, 0)),
        ],
        out_shape=[
            jax.ShapeDtypeStruct((t, QKV_W), BF16),
            jax.ShapeDtypeStruct((t, GATE_W), F32),
        ],
        compiler_params=_cparams(("arbitrary",), 56),
        name="inproj",
    )(x2, g, w_bf, cos_t, sin_up_t, sin_dn_t)


def _diff_attn_kernel(lam_init, q_ref, k_ref, v_ref, lq1_ref, lk1_ref, lq2_ref, lk2_ref,
                      g_ref, o_ref):
    q = q_ref[...]
    k = k_ref[...]
    lane = lax.broadcasted_iota(jnp.int32, q.shape, 1)
    zero = jnp.zeros_like(q)
    q1 = jnp.where(lane < DIFF_HEAD_DIM, q, zero)
    q2 = jnp.where(lane >= DIFF_HEAD_DIM, q, zero)
    nt = (((1,), (1,)), ((), ()))
    s1 = lax.dot_general(q1, k, nt, preferred_element_type=F32)
    s2 = lax.dot_general(q2, k, nt, preferred_element_type=F32)
    e1 = jnp.exp(s1 - jnp.max(s1, axis=-1, keepdims=True))
    e2 = jnp.exp(s2 - jnp.max(s2, axis=-1, keepdims=True))
    l1 = jnp.sum(e1, axis=-1, keepdims=True)
    l2 = jnp.sum(e2, axis=-1, keepdims=True)
    lam = (jnp.exp(jnp.sum(lq1_ref[...] * lk1_ref[...], axis=-1, keepdims=True))
           - jnp.exp(jnp.sum(lq2_ref[...] * lk2_ref[...], axis=-1, keepdims=True))
           + lam_init)
    v = v_ref[...]
    o = (jnp.dot(e1.astype(BF16), v, preferred_element_type=F32) * (1.0 / l1)
         - jnp.dot(e2.astype(BF16), v, preferred_element_type=F32) * (lam / l2))
    ms = jnp.mean(o * o, axis=-1, keepdims=True)
    o = o * lax.rsqrt(ms + EPS) * g_ref[...] * (1.0 - lam_init)
    o_ref[...] = o.astype(o_ref.dtype)


def _diff_attn(qkv3, lq1, lk1, lq2, lk2, subln_g, lam_init):
    b, s, _ = qkv3.shape
    tq = TQ_DIFF
    hw = 2 * DIFF_HEAD_DIM
    k_blk0 = A_QK_W // hw
    v_blk0 = 2 * A_QK_W // hw
    vec = pl.BlockSpec((1, DIFF_HEAD_DIM), lambda bi, h, i: (0, 0))
    return pl.pallas_call(
        functools.partial(_diff_attn_kernel, lam_init),
        grid=(b, N_DIFF_HEADS, s // tq),
        in_specs=[
            pl.BlockSpec((None, tq, hw), lambda bi, h, i: (bi, i, h)),
            pl.BlockSpec((None, s, hw), lambda bi, h, i: (bi, 0, k_blk0 + h)),
            pl.BlockSpec((None, s, DIFF_V_DIM), lambda bi, h, i: (bi, 0, v_blk0 + h)),
            vec, vec, vec, vec,
            pl.BlockSpec((1, DIFF_V_DIM), lambda bi, h, i: (0, 0)),
        ],
        out_specs=pl.BlockSpec((None, tq, DIFF_V_DIM), lambda bi, h, i: (bi, i, h)),
        out_shape=jax.ShapeDtypeStruct((b, s, A_V_W), BF16),
        compiler_params=_cparams(("arbitrary", "arbitrary", "arbitrary"), 48),
        name="diff_attn",
    )(qkv3, qkv3, qkv3, lq1, lk1, lq2, lk2, subln_g)


def _na_window_start(r, rows, kr):
    return jnp.clip(r - kr // 2, 0, rows - kr)


def _na_kernel(rows, kr, q_ref, k_ref, v_ref, bias_ref, o_ref):
    r = pl.program_id(1)
    start = pl.multiple_of(_na_window_start(r, rows, kr) * GRID_W, GRID_W)
    nkeys = kr * GRID_W
    nb = q_ref.shape[0]
    nt = (((1,), (1,)), ((), ()))
    pair_w = 2 * NA_HEAD_DIM
    lane = lax.broadcasted_iota(jnp.int32, (GRID_W, pair_w), 1)
    scores, vwins = [], []
    for bb in range(nb):
        kwin = k_ref[bb, pl.ds(start, nkeys), :]
        vwins.append(v_ref[bb, pl.ds(start, nkeys), :])
        q = q_ref[bb] * (NA_HEAD_DIM ** -0.5)
        for h in range(N_NA_HEADS):
            sl = slice((h // 2) * pair_w, (h // 2 + 1) * pair_w)
            keep = (lane < NA_HEAD_DIM) if h % 2 == 0 else (lane >= NA_HEAD_DIM)
            qm = jnp.where(keep, q[:, sl], jnp.zeros_like(q[:, sl]))
            s = lax.dot_general(qm, kwin[:, sl], nt, preferred_element_type=F32)
            scores.append(s + bias_ref[h])
    s = jnp.concatenate(scores, axis=0)
    e = jnp.exp(s - jnp.max(s, axis=-1, keepdims=True))
    p = (e * (1.0 / jnp.sum(e, axis=-1, keepdims=True))).astype(BF16)
    for bb in range(nb):
        for hp in range(N_NA_HEADS // 2):
            sl = slice(hp * pair_w, (hp + 1) * pair_w)
            row0 = (bb * N_NA_HEADS + 2 * hp) * GRID_W
            outs = [jnp.dot(p[row0 + hh * GRID_W:row0 + (hh + 1) * GRID_W], vwins[bb][:, sl],
                            preferred_element_type=F32) for hh in range(2)]
            o_ref[bb, :, sl] = jnp.where(lane < NA_HEAD_DIM, outs[0], outs[1]).astype(o_ref.dtype)


def _na_attn(qkv3, bias_tab, rows, kr):
    b, s, _ = qkv3.shape
    q_blk = (2 * A_QK_W + A_V_W) // B_W
    nkeys = kr * GRID_W
    nb = NB_NA if b % NB_NA == 0 else 1

    def bias_idx(bi, r):
        return (0, r - _na_window_start(r, rows, kr), 0, 0)

    return pl.pallas_call(
        functools.partial(_na_kernel, rows, kr),
        grid=(b // nb, rows),
        in_specs=[
            pl.BlockSpec((nb, GRID_W, B_W), lambda bi, r: (bi, r, q_blk)),
            pl.BlockSpec((nb, s, B_W), lambda bi, r: (bi, 0, q_blk + 1)),
            pl.BlockSpec((nb, s, B_W), lambda bi, r: (bi, 0, q_blk + 2)),
            pl.BlockSpec((N_NA_HEADS, None, GRID_W, nkeys), bias_idx),
        ],
        out_specs=pl.BlockSpec((nb, GRID_W, B_W), lambda bi, r: (bi, r, 0)),
        out_shape=jax.ShapeDtypeStruct((b, s, B_W), BF16),
        compiler_params=_cparams(("arbitrary", "arbitrary"), 56),
        name="na_attn",
    )(qkv3, qkv3, qkv3, bias_tab)


def _na_bias_table(rpb, rows, kr):
    w = np.arange(GRID_W)[:, None]
    c = np.arange(GRID_W)[None, :]
    cs = np.clip(w - NA_KC // 2, 0, GRID_W - NA_KC)
    valid = (c >= cs) & (c < cs + NA_KC)
    ncol = 2 * NA_KC - 1
    onehot = ((c - w + (NA_KC - 1))[None] == np.arange(ncol)[:, None, None]) & valid[None]
    top = NA_KR_MAX - 1
    rsel = jnp.stack([rpb[:, top - d:top - d + kr, :] for d in range(kr)], axis=1).astype(F32)
    tab = jnp.einsum('hdik,kwc->hdwic', rsel, jnp.asarray(onehot, F32),
                     precision=lax.Precision.HIGHEST)
    tab = jnp.where(valid[None, None, :, None, :], tab, NEG_BIG)
    return tab.reshape(rpb.shape[0], kr, GRID_W, kr * GRID_W)


def _merge_kernel(ya_ref, yb_ref, gate_ref, x_ref, wpa_ref, wpb_ref, wout_ref, g_ref, wq_ref,
                  keys_ref, h_ref, xnt_ref, st_ref):
    ma = jnp.dot(ya_ref[...], wpa_ref[...], preferred_element_type=F32)
    mb = jnp.dot(yb_ref[...], wpb_ref[...], preferred_element_type=F32)
    ga = gate_ref[:, :D_MODEL]
    gb = gate_ref[:, D_MODEL:]
    merged = jax.nn.sigmoid(ga) * ma + jax.nn.sigmoid(gb) * mb
    h = x_ref[...] + jnp.dot(merged.astype(BF16), wout_ref[...], preferred_element_type=F32)
    h_ref[...] = h
    ms = jnp.mean(h * h, axis=-1, keepdims=True)
    xn_f32 = h * lax.rsqrt(ms + EPS) * g_ref[...]
    xn = xn_f32.astype(BF16)
    xnt_ref[...] = xn_f32.T.astype(BF16)
    q = jnp.dot(xn, wq_ref[...], preferred_element_type=F32).astype(BF16)
    nt = (((1,), (1,)), ((), ()))
    half = PEER_QUERY_DIM // 2
    for hp in range(2 * PEER_HEADS):
        st_ref[hp % 2, :, hp // 2, :] = lax.dot_general(
            keys_ref[hp], q[:, hp * half:(hp + 1) * half], nt, preferred_element_type=F32)


def _merge(ya, yb, gates, x2, wpa, wpb, wout, g_ffn, wq, keys):
    t = x2.shape[0]
    tm = TM_MERGE
    qw = wq.shape[1]
    nlist = keys.shape[0]
    full = lambda shape: pl.BlockSpec(shape, lambda i: (0,) * len(shape),
                                      pipeline_mode=pl.Buffered(1))
    return pl.pallas_call(
        _merge_kernel,
        grid=(t // tm,),
        in_specs=[
            pl.BlockSpec((tm, A_V_W), lambda i: (i, 0)),
            pl.BlockSpec((tm, B_W), lambda i: (i, 0)),
            pl.BlockSpec((tm, GATE_W), lambda i: (i, 0)),
            pl.BlockSpec((tm, D_MODEL), lambda i: (i, 0)),
            full((A_V_W, D_MODEL)), full((B_W, D_MODEL)), full((D_MODEL, D_MODEL)),
            full((1, D_MODEL)), full((D_MODEL, qw)), full(keys.shape),
        ],
        out_specs=[
            pl.BlockSpec((tm, D_MODEL), lambda i: (i, 0)),
            pl.BlockSpec((D_MODEL, tm), lambda i: (0, i)),
            pl.BlockSpec((2, PEER_N_KEYS, nlist // 2, tm), lambda i: (0, 0, 0, i)),
        ],
        out_shape=[
            jax.ShapeDtypeStruct((t, D_MODEL), F32),
            jax.ShapeDtypeStruct((D_MODEL, t), BF16),
            jax.ShapeDtypeStruct((2, PEER_N_KEYS, nlist // 2, t), F32),
        ],
        compiler_params=_cparams(("arbitrary",), 56),
        name="merge_query",
    )(ya, yb, gates, x2, wpa, wpb, wout, g_ffn, wq, keys)


def _staircase_cells():
    k = PEER_TOPK
    return [(r1, r2) for r1 in range(k) for r2 in range(k // (r1 + 1))]


def _first_max(items):
    level = list(items)
    while len(level) > 1:
        nxt = []
        for i in range(0, len(level) - 1, 2):
            (v0, i0), (v1, i1) = level[i], level[i + 1]
            later = v1 > v0
            nxt.append((jnp.where(later, v1, v0), jnp.where(later, i1, i0)))
        if len(level) % 2:
            nxt.append(level[-1])
        level = nxt
    return level[0]


def _sort_desc(vals):
    vals = list(vals)
    n = len(vals)
    size = 2
    while size <= n:
        stride = size // 2
        while stride >= 1:
            for i in range(n):
                j = i ^ stride
                if j > i:
                    hi, lo = jnp.maximum(vals[i], vals[j]), jnp.minimum(vals[i], vals[j])
                    vals[i], vals[j] = (hi, lo) if (i & size) == 0 else (lo, hi)
            stride //= 2
        size *= 2
    return vals


def _merge_top(a, b):
    n = len(a)
    vals = [jnp.maximum(a[i], b[n - 1 - i]) for i in range(n)]
    stride = n // 2
    while stride >= 1:
        for i in range(n):
            j = i ^ stride
            if j > i:
                vals[i], vals[j] = jnp.maximum(vals[i], vals[j]), jnp.minimum(vals[i], vals[j])
        stride //= 2
    return vals


def _top_sorted(vals, k):
    groups = [_sort_desc(vals[i:i + k]) for i in range(0, len(vals), k)]
    while len(groups) > 1:
        groups = [_merge_top(groups[i], groups[i + 1]) for i in range(0, len(groups), 2)]
    return groups[0]


def _staircase(as_, bs_, cand_ref, pick_ref):
    k = PEER_TOPK
    tile = as_[0].shape
    cells = _staircase_cells()
    for ci, (r1, r2) in enumerate(cells):
        cand_ref[ci] = as_[r1] + bs_[r2]
    top = as_[0] + bs_[0]

    def pick(it, carry):
        prev, z = carry
        items = []
        for ci, (r1, r2) in enumerate(cells):
            pos = r1 * k + r2
            c = jnp.where(prev == pos, -jnp.inf, cand_ref[ci])
            cand_ref[ci] = c
            items.append((c, pos))
        m, first = _first_max(items)
        pick_ref[pl.ds(it, 1)] = first[None]
        return first, z + jnp.exp(m - top)

    _, z = lax.fori_loop(0, k, pick, (jnp.full(tile, -1, jnp.int32), jnp.zeros(tile, F32)))
    picked_row = [pick_ref[it] // k for it in range(k)]
    counts = []
    for r1 in range(k):
        c = jnp.zeros(tile, F32)
        for it in range(k):
            c = c + jnp.where(picked_row[it] == r1, 1.0, 0.0)
        counts.append(c)
    return counts, z


def _select_kernel(s_ref, cnt_ref, ea_ref, rank2_ref, eb_ref, cur_ref, cand_ref, srt_ref,
                   first_ref, pick_ref):
    k = PEER_TOPK
    nk = s_ref.shape[1]
    tile = s_ref.shape[2:]

    as_ = _top_sorted([s_ref[0, key] for key in range(nk)], k)
    bs_ = _top_sorted([s_ref[1, key] for key in range(nk)], k)
    counts, z = _staircase(as_, bs_, cand_ref, pick_ref)
    inv_z = 1.0 / z
    n_top = [jnp.zeros(tile, F32), jnp.zeros(tile, F32)]
    for key in range(nk):
        a, b = s_ref[0, key], s_ref[1, key]
        cnt = jnp.zeros(tile, F32)
        rank = jnp.full(tile, float(nk), F32)
        for r in range(k - 1, -1, -1):
            cnt = jnp.where(a >= as_[r], counts[r], cnt)
            rank = jnp.where(b >= bs_[r], float(r), rank)
        n_top[0] = n_top[0] + jnp.where(a >= as_[k - 1], 1.0, 0.0)
        n_top[1] = n_top[1] + jnp.where(b >= bs_[k - 1], 1.0, 0.0)
        cnt_ref[key] = cnt
        rank2_ref[:, key, :] = rank
        ea_ref[key] = jnp.exp(a - as_[0])
        eb_ref[:, key, :] = jnp.exp(b - bs_[0]) * inv_z

    tied = jnp.zeros(tile, F32)
    for srt, n in ((as_, n_top[0]), (bs_, n_top[1])):
        tied = jnp.where(n != float(k), 1.0, tied)
        for r in range(k - 1):
            tied = jnp.where(srt[r] == srt[r + 1], 1.0, tied)
    any_tied = jnp.max(tied)

    @pl.when(any_tied > 0.0)
    def _():
        _select_exact(s_ref, cnt_ref, ea_ref, rank2_ref, eb_ref, cur_ref, cand_ref, srt_ref,
                      first_ref, pick_ref)


def _select_exact(s_ref, cnt_ref, ea_ref, rank2_ref, eb_ref, cur_ref, cand_ref, srt_ref,
                  first_ref, pick_ref):
    k = PEER_TOPK
    nk = s_ref.shape[1]
    tile = s_ref.shape[2:]
    none = jnp.full(tile, -1, jnp.int32)

    cur_ref[...] = s_ref[...]

    def extract(it, prev):
        winners = []
        for l in range(2):
            items = []
            for key in range(nk):
                c = jnp.where(prev[l] == key, -jnp.inf, cur_ref[l, key])
                cur_ref[l, key] = c
                items.append((c, key))
            m, first = _first_max(items)
            srt_ref[l, pl.ds(it, 1)] = m[None]
            first_ref[l, pl.ds(it, 1)] = first[None]
            winners.append(first)
        return tuple(winners)

    lax.fori_loop(0, k, extract, (none, none))
    as_ = [srt_ref[0, r] for r in range(k)]
    bs_ = [srt_ref[1, r] for r in range(k)]
    counts, z = _staircase(as_, bs_, cand_ref, pick_ref)

    first_a = [first_ref[0, r] for r in range(k)]
    first_b = [first_ref[1, r] for r in range(k)]
    inv_z = 1.0 / z
    for key in range(nk):
        cnt = jnp.zeros(tile, F32)
        rank = jnp.full(tile, float(nk), F32)
        for r in range(k):
            cnt = jnp.where(first_a[r] == key, counts[r], cnt)
            rank = jnp.where(first_b[r] == key, float(r), rank)
        cnt_ref[key] = cnt
        rank2_ref[:, key, :] = rank
        ea_ref[key] = jnp.exp(s_ref[0, key] - as_[0])
        eb_ref[:, key, :] = jnp.exp(s_ref[1, key] - bs_[0]) * inv_z


def _select(scores):
    _, nk, heads, t = scores.shape
    k = PEER_TOPK
    tab = pl.BlockSpec((nk, heads, LANES), lambda i: (0, 0, i))
    shape = jax.ShapeDtypeStruct((nk, heads, t), F32)
    htab = pl.BlockSpec((heads, nk, LANES), lambda i: (0, 0, i))
    hshape = jax.ShapeDtypeStruct((heads, nk, t), F32)
    return pl.pallas_call(
        _select_kernel,
        grid=(t // LANES,),
        in_specs=[pl.BlockSpec((2, nk, heads, LANES), lambda i: (0, 0, 0, i))],
        out_specs=[tab, tab, htab, htab],
        out_shape=[shape, shape, hshape, hshape],
        scratch_shapes=[
            pltpu.VMEM((2, nk, heads, LANES), F32),
            pltpu.VMEM((len(_staircase_cells()), heads, LANES), F32),
            pltpu.VMEM((2, k, heads, LANES), F32),
            pltpu.VMEM((2, k, heads, LANES), jnp.int32),
            pltpu.VMEM((k, heads, LANES), jnp.int32),
        ],
        compiler_params=_cparams(("arbitrary",), 32),
        name="peer_select",
    )(scores)


def _peer_gate_chunk(tc, j_rows, cnt_ref, ea_ref, tab_ref, at_ref, p_ref):
    cols = slice(tc * LANES, (tc + 1) * LANES)
    vr = GATE_VREG_ROWS
    n_k = PEER_N_KEYS // vr
    acc = [[None] * n_k for _ in j_rows]
    for h in range(PEER_HEADS):
        bcast = lambda ref, j: jnp.broadcast_to(ref[j, h:h + 1, cols], (vr, LANES)).astype(GATE_DTYPE)
        cnt_b = [bcast(cnt_ref, j) for j in j_rows]
        ea_b = [bcast(ea_ref, j) for j in j_rows]
        for k in range(n_k):
            r2 = tab_ref[h, tc, k * vr:(k + 1) * vr, :]
            ebh = tab_ref[h, tc, TAB_EB_ROW + k * vr:TAB_EB_ROW + (k + 1) * vr, :]
            for jj in range(len(j_rows)):
                term = jnp.where(r2 < cnt_b[jj], ebh, 0.0) * ea_b[jj]
                acc[jj][k] = term if acc[jj][k] is None else acc[jj][k] + term
    pack = BF16_SUBLANE_ROWS // vr
    for jj, j in enumerate(j_rows):
        for m in range(n_k // pack):
            r0 = j * PEER_N_KEYS + m * BF16_SUBLANE_ROWS
            rows = slice(r0, r0 + BF16_SUBLANE_ROWS)
            act = at_ref[rows, cols]
            gelu = 0.5 * act * (1.0 + lax.erf(act * math.sqrt(0.5)))
            w = jnp.concatenate(acc[jj][m * pack:(m + 1) * pack], axis=0)
            p_ref[rows, cols] = (w * gelu.astype(GATE_DTYPE)).astype(BF16)


def _peer_mix_kernel(final_norm, xnt_ref, u_ref, vt_ref, cnt_ref, ea_ref, rank2_ref, eb_ref,
                     h_ref, g_ref, o_ref, at_ref, p_ref, acc_ref, tab_ref):
    e = pl.program_id(1)
    eb, tt = u_ref.shape[0], xnt_ref.shape[1]

    @pl.when(e == 0)
    def _():
        acc_ref[...] = jnp.zeros_like(acc_ref)
        for h in range(PEER_HEADS):
            for tc in range(tt // LANES):
                cols = slice(tc * LANES, (tc + 1) * LANES)
                tab_ref[h, tc, 0:PEER_N_KEYS, :] = rank2_ref[h, :, cols].astype(GATE_DTYPE)
                tab_ref[h, tc, TAB_EB_ROW:TAB_EB_ROW + PEER_N_KEYS, :] = eb_ref[h, :, cols].astype(GATE_DTYPE)

    rows_per_sub = SB_PEER // PEER_N_KEYS
    for sb in range(eb // SB_PEER):
        sub = slice(sb * SB_PEER, (sb + 1) * SB_PEER)
        at_ref[sub, 0:tt] = jnp.dot(u_ref[sub, :], xnt_ref[...], preferred_element_type=F32)
        for jg in range(sb * rows_per_sub, (sb + 1) * rows_per_sub, GATE_ROWS):
            for tc in range(tt // LANES):
                _peer_gate_chunk(tc, tuple(range(jg, jg + GATE_ROWS)), cnt_ref, ea_ref, tab_ref,
                                 at_ref, p_ref)
    acc_ref[...] += jnp.dot(vt_ref[...], p_ref[:, 0:tt], preferred_element_type=F32)

    @pl.when(e == pl.num_programs(1) - 1)
    def _():
        y = h_ref[...] + acc_ref[...].T
        if final_norm:
            ms = jnp.mean(y * y, axis=-1, keepdims=True)
            y = y * lax.rsqrt(ms + EPS) * g_ref[...]
        o_ref[...] = y


def _peer_mix(xnt, u_bf, vt_bf, cnt, ea, rank2, eb, h, g_final, final_norm):
    t = xnt.shape[1]
    tt, ebk = TT_PEER, EB_PEER
    n_exp = u_bf.shape[0]
    nk, heads, _ = cnt.shape
    tab = pl.BlockSpec((nk, heads, tt), lambda ti, e: (0, 0, ti))
    row_tab = pl.BlockSpec((ebk // nk, heads, tt), lambda ti, e: (e, 0, ti))
    htab = pl.BlockSpec((heads, nk, tt), lambda ti, e: (0, 0, ti))
    return pl.pallas_call(
        functools.partial(_peer_mix_kernel, final_norm),
        grid=(t // tt, n_exp // ebk),
        in_specs=[
            pl.BlockSpec((D_MODEL, tt), lambda ti, e: (0, ti)),
            pl.BlockSpec((ebk, D_MODEL), lambda ti, e: (e, 0)),
            pl.BlockSpec((None, D_MODEL, ebk), lambda ti, e: (e, 0, 0)),
            row_tab, row_tab, htab, htab,
            pl.BlockSpec((tt, D_MODEL), lambda ti, e: (ti, 0)),
            pl.BlockSpec((1, D_MODEL), lambda ti, e: (0, 0)),
        ],
        out_specs=pl.BlockSpec((tt, D_MODEL), lambda ti, e: (ti, 0)),
        out_shape=jax.ShapeDtypeStruct((t, D_MODEL), F32),
        scratch_shapes=[
            pltpu.VMEM((ebk, tt + LANES), F32),
            pltpu.VMEM((ebk, tt + LANES), BF16),
            pltpu.VMEM((D_MODEL, tt), F32),
            pltpu.VMEM((heads, tt // LANES, TAB_ROWS, LANES), GATE_DTYPE),
        ],
        compiler_params=_cparams(("arbitrary", "arbitrary"), 56),
        name="peer_mix",
    )(xnt, u_bf, vt_bf, cnt, ea, rank2, eb, h, g_final)


def _rope_tables(seq):
    half = ROPE_DIM // 2
    inv_freq = ROPE_THETA ** (-jnp.arange(0, ROPE_DIM, 2, dtype=F32) / ROPE_DIM)
    ang = jnp.arange(seq, dtype=F32)[:, None] * inv_freq[None, :]
    cos, sin = jnp.cos(ang), jnp.sin(ang)
    pad = jnp.zeros((seq, DIFF_HEAD_DIM - ROPE_DIM), F32)
    c64 = jnp.concatenate([cos, cos, pad + 1.0], axis=1)
    up64 = jnp.concatenate([jnp.zeros_like(sin), sin, pad], axis=1)
    dn64 = jnp.concatenate([-sin, jnp.zeros_like(sin), pad], axis=1)
    rep = LANES // DIFF_HEAD_DIM
    return jnp.tile(c64, (1, rep)), jnp.tile(up64, (1, rep)), jnp.tile(dn64, (1, rep))


def kernel(x, w_in, w_proj_a, w_proj_b, w_out, norm_mix, norm_ffn, norm_final, lambda_q1,
           lambda_k1, lambda_q2, lambda_k2, diff_subln, na_rpb, peer_w_query, peer_sub_keys,
           peer_u, peer_v):
    b, s, d = x.shape
    t = b * s
    depth = w_in.shape[0]
    rows = s // GRID_W
    kr = min(NA_KR_MAX, rows)
    cos_t, sin_up_t, sin_dn_t = _rope_tables(s)
    h = x.reshape(t, d)
    row = lambda v: v.reshape(1, -1).astype(F32)
    for l in range(depth):
        lam_init = 0.8 - 0.6 * math.exp(-0.3 * l)
        qkv, gates = _inproj(h, row(norm_mix[l]), w_in[l].astype(BF16), cos_t, sin_up_t,
                             sin_dn_t, s)
        qkv3 = qkv.reshape(b, s, QKV_W)
        ya = _diff_attn(qkv3, row(lambda_q1[l]), row(lambda_k1[l]), row(lambda_q2[l]),
                        row(lambda_k2[l]), row(diff_subln[l]), lam_init)
        yb = _na_attn(qkv3, _na_bias_table(na_rpb[l], rows, kr), rows, kr)
        keys = peer_sub_keys[l].reshape(2 * PEER_HEADS, PEER_N_KEYS, PEER_QUERY_DIM // 2)
        h_mid, xnt, scores_t = _merge(
            ya.reshape(t, A_V_W), yb.reshape(t, B_W), gates, h,
            w_proj_a[l].astype(BF16), w_proj_b[l].astype(BF16), w_out[l].astype(BF16),
            row(norm_ffn[l]), peer_w_query[l].astype(BF16), keys.astype(BF16))
        cnt, ea, rank2, eb = _select(scores_t)
        vt_blocks = peer_v[l].reshape(-1, EB_PEER, d).transpose(0, 2, 1).astype(BF16)
        h = _peer_mix(xnt, peer_u[l].astype(BF16), vt_blocks, cnt, ea, rank2, eb,
                      h_mid, row(norm_final), final_norm=(l == depth - 1))
    return h.reshape(b, s, d)
```
